```python
import math
import jax, jax.numpy as jnp
from jax import lax
import numpy as np

D_MODEL = 1024
BATCH = 16
SEQ = 256
DEPTH = 2
DEC_BATCH = 2
DEC_SEQ = 2048
PAST_LEN = 256

GRID_W = 64
BR_W = 512
N_BRANCH = 4
N_HEADS = 8
N_KV = 2
HEAD_DIM = 64
Q_PER_KV = N_HEADS // N_KV
WINDOW = 128
BLOCK = 128
ROPE_THETA = 10000.0
NEG_INF = -1e30
CONV_K = 31
POOL_WINDOWS = (2, 4, 8, 16)
POOL_GROUPS = 4
POOL_GW = BR_W // POOL_GROUPS
SSM_CH = 16
SSM_GROUPS = BR_W // SSM_CH
SSM_P = 64
EPS = 1e-6

IN_SIZES = (N_HEADS * HEAD_DIM, N_KV * HEAD_DIM, N_KV * HEAD_DIM, BR_W,
            2 * BR_W, BR_W,
            BR_W, BR_W,
            BR_W, BR_W,
            N_BRANCH * D_MODEL)
IN_COLS = sum(IN_SIZES)

kernel_name = "hybrid_diffusion_prefix_step"

f32 = jnp.float32


def _rmsnorm(x, g):
    xf = x.astype(f32)
    xf = xf * lax.rsqrt(jnp.mean(xf * xf, axis=-1, keepdims=True) + EPS)
    return xf.astype(x.dtype) * g


def _layernorm(x, g, b):
    xf = x.astype(f32)
    mu = jnp.mean(xf, axis=-1, keepdims=True)
    xc = xf - mu
    xn = xc * lax.rsqrt(jnp.mean(xc * xc, axis=-1, keepdims=True) + EPS)
    return xn.astype(x.dtype) * g + b


def _split_cols(u):
    idx, s = [], 0
    for n in IN_SIZES[:-1]:
        s += n
        idx.append(s)
    return jnp.split(u, idx, axis=-1)


def _axial_rope(x):
    L = x.shape[1]
    rows = L // GRID_W
    row = jnp.repeat(jnp.arange(rows, dtype=f32), GRID_W)
    col = jnp.tile(jnp.arange(GRID_W, dtype=f32), rows)
    n_freq = HEAD_DIM // 4
    inv = ROPE_THETA ** (-jnp.arange(n_freq, dtype=f32) / n_freq)
    ang = jnp.concatenate([row[:, None] * inv, col[:, None] * inv], axis=-1)
    shape = (1, L) + (1,) * (x.ndim - 3) + (HEAD_DIM // 2,)
    cos = jnp.cos(ang).reshape(shape)
    sin = jnp.sin(ang).reshape(shape)
    xf = x.astype(f32)
    x1, x2 = xf[..., :HEAD_DIM // 2], xf[..., HEAD_DIM // 2:]
    return jnp.concatenate([x1 * cos - x2 * sin, x1 * sin + x2 * cos], axis=-1).astype(x.dtype)


def _attend(q, kv_groups, sink):
    scale = HEAD_DIM ** -0.5
    scores = []
    for k, v, mask in kv_groups:
        s = jnp.einsum('bqkgd,bskd->bkgqs', q, k).astype(f32) * scale
        if mask is not None:
            s = jnp.where(mask, s, NEG_INF)
        scores.append(s)
    sink_col = jnp.broadcast_to(sink.astype(f32).reshape(N_KV, Q_PER_KV, 1, 1), scores[0].shape[:-1] + (1,))
    p = jax.nn.softmax(jnp.concatenate(scores + [sink_col], axis=-1), axis=-1)
    out, start = None, 0
    for (k, v, _), s in zip(kv_groups, scores):
        n = s.shape[-1]
        o = jnp.einsum('bkgqs,bskd->bqkgd', p[..., start:start + n].astype(v.dtype), v)
        out = o if out is None else out + o
        start += n
    return out


def _context_attention(q, k, v, sink):
    B, S = q.shape[:2]
    nb = S // BLOCK
    qb = q.reshape(B, nb, BLOCK, N_KV, Q_PER_KV, HEAD_DIM).transpose(1, 0, 2, 3, 4, 5)
    out = lax.map(lambda qi: _attend(qi, [(k, v, None)], sink), qb)
    return out.transpose(1, 0, 2, 3, 4, 5).reshape(B, S, N_HEADS * HEAD_DIM)


def _latent_attention(q, k, v, ck, cv, sink):
    B, L = q.shape[:2]
    nb = L // BLOCK
    pad = ((0, 0), (BLOCK, BLOCK), (0, 0), (0, 0))
    kp, vp = jnp.pad(k, pad), jnp.pad(v, pad)

    def block(i):
        q_i = lax.dynamic_slice_in_dim(q, i * BLOCK, BLOCK, axis=1)
        k_i = lax.dynamic_slice_in_dim(kp, i * BLOCK, 3 * BLOCK, axis=1)
        v_i = lax.dynamic_slice_in_dim(vp, i * BLOCK, 3 * BLOCK, axis=1)
        qpos = i * BLOCK + jnp.arange(BLOCK)
        kpos = (i - 1) * BLOCK + jnp.arange(3 * BLOCK)
        mask = (jnp.abs(qpos[:, None] - kpos[None, :]) <= WINDOW) & (kpos >= 0)[None, :] & (kpos < L)[None, :]
        return _attend(q_i, [(ck, cv, None), (k_i, v_i, mask)], sink)

    out = lax.map(block, jnp.arange(nb))
    return out.transpose(1, 0, 2, 3, 4, 5).reshape(B, L, N_HEADS * HEAD_DIM)


def _conformer_conv(a, w_dw, b_dw, ln_g, ln_b):
    x = a[..., :BR_W] * jax.nn.sigmoid(a[..., BR_W:])
    y = lax.conv_general_dilated(x.astype(w_dw.dtype), w_dw[:, None, :], window_strides=(1,),
                                 padding=[(CONV_K // 2, CONV_K // 2)],
                                 dimension_numbers=('NWC', 'WIO', 'NWC'),
                                 feature_group_count=BR_W) + b_dw
    return jax.nn.silu(_layernorm(y, ln_g, ln_b))


def _multiscale_pool(x, w_pool, scale):
    B, L, _ = x.shape
    xg = x.reshape(B, L, POOL_GROUPS, POOL_GW).astype(f32)
    cs = jnp.pad(jnp.cumsum(xg, axis=1), ((0, 0), (1, 0), (0, 0), (0, 0)))
    t = jnp.arange(L)
    pooled = []
    for g, w in enumerate(POOL_WINDOWS):
        lo = jnp.clip(t - w // 2, 0, L)
        hi = jnp.clip(t + w - w // 2, 0, L)
        cnt = (hi - lo).astype(f32)[None, :, None]
        pooled.append((cs[:, hi, g] - cs[:, lo, g]) / cnt)
    d = (jnp.stack(pooled, axis=2) - xg).astype(x.dtype)
    y = jnp.einsum('blgc,gcd->blgd', d, w_pool)
    return y.reshape(B, L, BR_W) * scale


def _lin_combine(e1, e2):
    a1, b1 = e1
    a2, b2 = e2
    return a1 * a2, a2 * b1 + b2


def _diag_scan(xg, lam_bar, b_bar, c_re, c_im, h0):
    bu = lax.complex(jnp.einsum('blgc,gpc->blgp', xg, jnp.real(b_bar)),
                     jnp.einsum('blgc,gpc->blgp', xg, jnp.imag(b_bar)))
    if h0 is not None:
        bu = bu.at[:, 0].add(lam_bar * h0)
    a = jnp.broadcast_to(lam_bar, bu.shape)
    _, h = lax.associative_scan(_lin_combine, (a, bu), axis=1)
    y = jnp.einsum('blgp,gcp->blgc', jnp.real(h), c_re) - jnp.einsum('blgp,gcp->blgc', jnp.imag(h), c_im)
    return y, h[:, -1]


def _ssm_branch(x, p, state):
    B, L, _ = x.shape
    xg = x.astype(f32).reshape(B, L, SSM_GROUPS, SSM_CH)
    y_sum, finals = None, []
    for d in range(2):
        lam = lax.complex(p['lam_re'][d].astype(f32), p['lam_im'][d].astype(f32))
        dt = jnp.exp(p['log_dt'][d].astype(f32))[:, None]
        lam_bar = jnp.exp(lam * dt)
        bmat = lax.complex(p['b_re'][d].astype(f32), p['b_im'][d].astype(f32))
        b_bar = ((lam_bar - 1.0) / lam)[..., None] * bmat
        h0 = None if state is None else lax.complex(state[:, d, 0].astype(f32), state[:, d, 1].astype(f32))
        xd = xg if d == 0 else jnp.flip(xg, axis=1)
        y, h_last = _diag_scan(xd, lam_bar, b_bar, p['c_re'][d].astype(f32), p['c_im'][d].astype(f32), h0)
        if d == 1:
            y = jnp.flip(y, axis=1)
        y_sum = y if y_sum is None else y_sum + y
        finals.append(h_last)
    y = y_sum + p['ssm_d'].astype(f32).reshape(SSM_GROUPS, SSM_CH) * xg
    y = jax.nn.gelu(y.reshape(B, L, BR_W)).astype(x.dtype)
    z = y @ p['glu_w']
    out = z[..., :BR_W] * jax.nn.sigmoid(z[..., BR_W:])
    if state is None:
        packed = jnp.stack([jnp.stack([jnp.real(h), jnp.imag(h)], axis=1) for h in finals], axis=1)
        return out, packed
    return out, None


def _layer(x, mod, p, cache):
    B, L, _ = x.shape
    shift, scale, gate = jnp.split(mod, 3, axis=-1)
    h = _rmsnorm(x, p['norm_g']) * (1.0 + scale) + shift
    u = h @ p['w_in']
    q, k, v, g_att, a_conv, g_conv, x_pool, g_pool, x_ssm, g_ssm, g_mrg = _split_cols(u)
    q = q.reshape(B, L, N_KV, Q_PER_KV, HEAD_DIM)
    k = k.reshape(B, L, N_KV, HEAD_DIM)
    v = v.reshape(B, L, N_KV, HEAD_DIM)
    if cache is None:
        o_att = _context_attention(q, k, v, p['sink'])
        o_ssm, st = _ssm_branch(x_ssm, p, None)
        new = (k, v, st)
    else:
        ck, cv, st0 = cache
        o_att = _latent_attention(_axial_rope(q), _axial_rope(k), v, ck, cv, p['sink'])
        o_ssm, _ = _ssm_branch(x_ssm, p, st0)
        new = None
    o_conv = _conformer_conv(a_conv, p['conv_dw'], p['conv_db'], p['conv_ln_g'], p['conv_ln_b'])
    o_pool = _multiscale_pool(x_pool, p['pool_w'], p['pool_scale'])
    br = jnp.stack([o_att * jax.nn.silu(g_att), o_conv * jax.nn.silu(g_conv),
                    o_pool * jax.nn.silu(g_pool), o_ssm * jax.nn.silu(g_ssm)], axis=2)
    proj = jnp.einsum('blnw,nwd->blnd', br, p['w_br'])
    gates = jax.nn.sigmoid(g_mrg.reshape(B, L, N_BRANCH, D_MODEL))
    merged = jnp.sum(gates * proj, axis=2)
    return x + gate * (merged @ p['w_out']), new


def setup_inputs(seed: int = 0) -> dict:
    key = jax.random.key(seed)
    ks = jax.random.split(key, 32)
    W, G, P, C, D = BR_W, SSM_GROUPS, SSM_P, SSM_CH, D_MODEL

    def nrm(k, shape, s):
        return jax.random.normal(k, shape, f32) * s

    return {
        "x_prompt": nrm(ks[0], (BATCH, SEQ, D), 1.0),
        "x_sample": nrm(ks[1], (DEC_BATCH, DEC_SEQ, D), 1.0),
        "cache_k": nrm(ks[2], (DEC_BATCH, DEPTH, PAST_LEN, N_KV, HEAD_DIM), 1.0),
        "cache_v": nrm(ks[3], (DEC_BATCH, DEPTH, PAST_LEN, N_KV, HEAD_DIM), 1.0),
        "state_ssm": nrm(ks[4], (DEC_BATCH, DEPTH, 2, 2, G, P), 0.3),
        "c": nrm(ks[5], (DEC_BATCH, D), 1.0),
        "c_ctx": nrm(ks[6], (D,), 1.0),
        "norm_g": 1.0 + nrm(ks[7], (DEPTH, D), 0.02),
        "w_ada": nrm(ks[8], (DEPTH, D, 3 * D), 0.5 * D ** -0.5),
        "b_ada": nrm(ks[9], (DEPTH, 3 * D), 0.02),
        "w_in": nrm(ks[10], (DEPTH, D, IN_COLS), D ** -0.5),
        "attn_sink": nrm(ks[11], (DEPTH, N_HEADS), 0.5),
        "conv_dw": nrm(ks[12], (DEPTH, CONV_K, W), CONV_K ** -0.5),
        "conv_db": nrm(ks[13], (DEPTH, W), 0.02),
        "conv_ln_g": 1.0 + nrm(ks[14], (DEPTH, W), 0.02),
        "conv_ln_b": nrm(ks[15], (DEPTH, W), 0.02),
        "pool_w": nrm(ks[16], (DEPTH, POOL_GROUPS, POOL_GW, POOL_GW), POOL_GW ** -0.5),
        "pool_scale": 1.0 + nrm(ks[17], (DEPTH, W), 0.1),
        "ssm_lam_re": -0.5 + nrm(ks[18], (DEPTH, 2, G, P), 0.01),
        "ssm_lam_im": math.pi * jnp.arange(P, dtype=f32) + nrm(ks[19], (DEPTH, 2, G, P), 0.01),
        "ssm_log_dt": jax.random.uniform(ks[20], (DEPTH, 2, G), f32, math.log(1e-3), math.log(1e-1)),
        "ssm_b_re": nrm(ks[21], (DEPTH, 2, G, P, C), (2 * C) ** -0.5),
        "ssm_b_im": nrm(ks[22], (DEPTH, 2, G, P, C), (2 * C) ** -0.5),
        "ssm_c_re": nrm(ks[23], (DEPTH, 2, G, C, P), (2 * P) ** -0.5),
        "ssm_c_im": nrm(ks[24], (DEPTH, 2, G, C, P), (2 * P) ** -0.5),
        "ssm_d": nrm(ks[25], (DEPTH, W), 1.0),
        "ssm_glu_w": nrm(ks[26], (DEPTH, W, 2 * W), W ** -0.5),
        "w_br": nrm(ks[27], (DEPTH, N_BRANCH, W, D), W ** -0.5),
        "w_out": nrm(ks[28], (DEPTH, D, D), D ** -0.5),
        "final_g": 1.0 + nrm(ks[29], (D,), 0.02),
    }


def reference(x_prompt, x_sample, cache_k, cache_v, state_ssm, c, c_ctx, norm_g, w_ada, b_ada, w_in,
              attn_sink, conv_dw, conv_db, conv_ln_g, conv_ln_b, pool_w, pool_scale, ssm_lam_re, ssm_lam_im,
              ssm_log_dt, ssm_b_re, ssm_b_im, ssm_c_re, ssm_c_im, ssm_d, ssm_glu_w, w_br, w_out, final_g):
    xp, xs = x_prompt, x_sample
    new_k, new_v, new_st = [], [], []
    for l in range(DEPTH):
        p = {
            'norm_g': norm_g[l], 'w_in': w_in[l], 'sink': attn_sink[l],
            'conv_dw': conv_dw[l], 'conv_db': conv_db[l], 'conv_ln_g': conv_ln_g[l], 'conv_ln_b': conv_ln_b[l],
            'pool_w': pool_w[l], 'pool_scale': pool_scale[l],
            'lam_re': ssm_lam_re[l], 'lam_im': ssm_lam_im[l], 'log_dt': ssm_log_dt[l],
            'b_re': ssm_b_re[l], 'b_im': ssm_b_im[l], 'c_re': ssm_c_re[l], 'c_im': ssm_c_im[l],
            'ssm_d': ssm_d[l], 'glu_w': ssm_glu_w[l], 'w_br': w_br[l], 'w_out': w_out[l],
        }
        mod_p = (jax.nn.silu(c_ctx) @ w_ada[l] + b_ada[l])[None, None, :]
        mod_s = (jax.nn.silu(c) @ w_ada[l] + b_ada[l])[:, None, :]
        xp, (k_l, v_l, st_l) = _layer(xp, mod_p, p, None)
        new_k.append(k_l)
        new_v.append(v_l)
        new_st.append(st_l)
        xs, _ = _layer(xs, mod_s, p, (cache_k[:, l], cache_v[:, l], state_ssm[:, l]))
    y_prompt = _rmsnorm(xp, final_g)
    y_sample = _rmsnorm(xs, final_g)
    new_cache_k = jnp.stack(new_k, axis=1)
    new_cache_v = jnp.stack(new_v, axis=1)
    new_state_ssm = jnp.stack(new_st, axis=1)
    return (y_prompt, y_sample, new_cache_k, new_cache_v, new_state_ssm)
```

```python
import functools
import math

import jax
import jax.numpy as jnp
from jax import lax
from jax.experimental import pallas as pl
from jax.experimental.pallas import tpu as pltpu

f32 = jnp.float32
bf16 = jnp.bfloat16
HIGHEST = lax.Precision.HIGHEST

D_MODEL = 1024
DEPTH = 2
GRID_W = 64
BR_W = 512
N_BRANCH = 4
N_HEADS = 8
N_KV = 2
HEAD_DIM = 64
Q_PER_KV = N_HEADS // N_KV
KV_W = N_KV * HEAD_DIM
WINDOW = 128
ROPE_THETA = 10000.0
NEG_INF = -1e30
CONV_K = 31
CONV_HALO = 16
POOL_WINDOWS = (2, 4, 8, 16)
POOL_GROUPS = 4
POOL_GW = BR_W // POOL_GROUPS
POOL_HALO = 8
SSM_CH = 16
SSM_GROUPS = BR_W // SSM_CH
SSM_P = 64
SSM_T = 16
SSM_TC = SSM_T * SSM_CH
EPS = 1e-6

IN_SIZES = (N_HEADS * HEAD_DIM, KV_W, KV_W, BR_W, 2 * BR_W, BR_W, BR_W, BR_W, BR_W, BR_W, N_BRANCH * D_MODEL)
IN_NAMES = ("q", "k", "v", "g_att", "a_conv", "g_conv", "x_pool", "g_pool", "x_ssm", "g_ssm", "g_mrg")
IN_OFF = {}
_o = 0
for _n, _s in zip(IN_NAMES, IN_SIZES):
    IN_OFF[_n] = (_o, _o + _s)
    _o += _s

TM = 256
MOD_ROWS = 8
V7X_VMEM_LIMIT = 56 * 1024 * 1024


def _silu(x):
    return x * jax.nn.sigmoid(x)


def _modnorm(x, g, mod):
    shift = mod[:, :D_MODEL]
    scale = mod[:, D_MODEL:2 * D_MODEL]
    xn = x * lax.rsqrt(jnp.mean(x * x, axis=-1, keepdims=True) + EPS)
    return (xn * g) * (1.0 + scale) + shift


def _dot(a, b):
    return jnp.dot(a, b, preferred_element_type=f32)


def _const_spec(shape):
    nd = len(shape)
    return pl.BlockSpec(shape, lambda *_: (0,) * nd, pipeline_mode=pl.Buffered(1))


def _mod_kernel(c_ref, w_ref, b_ref, o_ref):
    s = _silu(c_ref[...])
    o_ref[0] = jnp.dot(s, w_ref[0], precision=HIGHEST, preferred_element_type=f32) + b_ref[0]


def _mod_call(c_rows, w_ada, b_ada):
    nb = 512
    return pl.pallas_call(
        _mod_kernel,
        grid=(DEPTH, 3 * D_MODEL // nb),
        in_specs=[
            pl.BlockSpec((MOD_ROWS, D_MODEL), lambda l, j: (0, 0)),
            pl.BlockSpec((1, D_MODEL, nb), lambda l, j: (l, 0, j)),
            pl.BlockSpec((1, 1, nb), lambda l, j: (l, 0, j)),
        ],
        out_specs=pl.BlockSpec((1, MOD_ROWS, nb), lambda l, j: (l, 0, j)),
        out_shape=jax.ShapeDtypeStruct((DEPTH, MOD_ROWS, 3 * D_MODEL), f32),
        name="mod",
    )(c_rows, w_ada, b_ada.reshape(DEPTH, 1, 3 * D_MODEL))


def _rope(x, cos, sin_signed):
    width = x.shape[-1]
    lane = lax.broadcasted_iota(jnp.int32, x.shape, 1)
    first_half = (lane & (HEAD_DIM - 1)) < HEAD_DIM // 2
    from_up = pltpu.roll(x, width - HEAD_DIM // 2, 1)
    from_dn = pltpu.roll(x, HEAD_DIM // 2, 1)
    return x * cos + jnp.where(first_half, from_up, from_dn) * sin_signed


def _p1_kernel(*refs, rope):
    if rope:
        x_ref, g_ref, mod_ref, w_ref, cos_ref, sin_ref, k_ref, v_ref, xc_ref, xp_ref, xs_ref = refs
    else:
        x_ref, g_ref, mod_ref, w_ref, k_ref, v_ref, xc_ref, xp_ref, xs_ref = refs
    h = _modnorm(x_ref[...], g_ref[...], mod_ref[...]).astype(bf16)
    kv = _dot(h, w_ref[:, 0:2 * KV_W])
    k = kv[:, :KV_W]
    if rope:
        k = _rope(k, cos_ref[...], sin_ref[...])
    k_ref[...] = k
    v_ref[...] = kv[:, KV_W:]
    o = 2 * KV_W
    a = _dot(h, w_ref[:, o:o + 2 * BR_W])
    xc_ref[...] = a[:, :BR_W] * jax.nn.sigmoid(a[:, BR_W:])
    o += 2 * BR_W
    xp_ref[...] = _dot(h, w_ref[:, o:o + BR_W])
    o += BR_W
    xs_ref[...] = _dot(h, w_ref[:, o:o + BR_W])


def _p1_call(x, norm_g, mod, w1, layer, latent, rope_k):
    B, L, _ = x.shape
    nt = L // TM
    if latent:
        mod_idx = lambda b, j: (layer * MOD_ROWS + 1 + b, 0, 0)
    else:
        mod_idx = lambda b, j: (layer * MOD_ROWS, 0, 0)
    row = lambda w: pl.BlockSpec((None, TM, w), lambda b, j: (b, j, 0))
    in_specs = [
        row(D_MODEL),
        _const_spec((1, D_MODEL)),
        pl.BlockSpec((None, 1, 3 * D_MODEL), mod_idx),
        _const_spec(w1.shape),
    ]
    args = [x, norm_g, mod, w1]
    if latent:
        in_specs += [pl.BlockSpec((TM, KV_W), lambda b, j: (j, 0))] * 2
        args += list(rope_k)
    shp = lambda w: jax.ShapeDtypeStruct((B, L, w), f32)
    return pl.pallas_call(
        functools.partial(_p1_kernel, rope=latent),
        grid=(B, nt),
        in_specs=in_specs,
        out_specs=[row(KV_W), row(KV_W), row(BR_W), row(BR_W), row(BR_W)],
        out_shape=[shp(KV_W), shp(KV_W), shp(BR_W), shp(BR_W), shp(BR_W)],
        compiler_params=pltpu.CompilerParams(
            dimension_semantics=("arbitrary", "arbitrary"), vmem_limit_bytes=V7X_VMEM_LIMIT),
        name="p1_latent" if latent else "p1_context",
    )(*args)


def _cpow(lr, li, dt, tau):
    mag = jnp.exp(lr * dt * tau)
    ang = li * dt * tau
    return mag * jnp.cos(ang), mag * jnp.sin(ang)


def _ssm_prep_kernel(ldt_ref, lrc_ref, lic_ref, lrr_ref, lir_ref, btr_ref, bti_ref, ctr_ref, cti_ref,
                     km_ref, bpr_ref, bpi_ref, cpr_ref, cpi_ref, ltr_ref, lti_ref):
    lane = lax.broadcasted_iota(jnp.int32, (SSM_P, SSM_TC), 1)
    lane_row = lax.broadcasted_iota(jnp.int32, (SSM_CH, SSM_TC), 1)
    blk = jnp.right_shift(lane, int(math.log2(SSM_CH))).astype(f32)
    for d in range(2):
        dt = jnp.exp(ldt_ref[d])
        lr_c, li_c = lrc_ref[d], lic_ref[d]
        lr_r, li_r = lrr_ref[d], lir_ref[d]
        a_re, a_im = _cpow(lr_r, li_r, dt, 1.0)
        den = lr_r * lr_r + li_r * li_r
        w_re = ((a_re - 1.0) * lr_r + a_im * li_r) / den
        w_im = (a_im * lr_r - (a_re - 1.0) * li_r) / den
        bt_re, bt_im = btr_ref[d], bti_ref[d]
        bb_re = w_re * bt_re - w_im * bt_im
        bb_im = w_re * bt_im + w_im * bt_re
        ct_re, ct_im = ctr_ref[d], cti_ref[d]
        tau = blk if d == 0 else (SSM_T - 1.0) - blk
        p_re, p_im = _cpow(lr_c, li_c, dt, tau)
        w_re2 = p_re * ct_re - p_im * ct_im
        w_im2 = p_re * ct_im + p_im * ct_re
        grow = (jnp.dot(bb_re, w_re2, precision=HIGHEST, preferred_element_type=f32)
                - jnp.dot(bb_im, w_im2, precision=HIGHEST, preferred_element_type=f32))
        q_re, q_im = _cpow(lr_c, li_c, dt, tau + 1.0)
        cpr_ref[d] = (q_re * ct_re - q_im * ct_im).astype(bf16)
        cpi_ref[d] = (-(q_re * ct_im + q_im * ct_re)).astype(bf16)
        for s in range(SSM_T):
            if d == 0:
                shift = SSM_CH * s
                keep = lane_row >= shift
            else:
                shift = (SSM_CH * (s + 1)) % SSM_TC
                keep = lane_row < SSM_CH * (s + 1)
            rolled = pltpu.roll(grow, shift, 1) if shift else grow
            km_ref[d, s * SSM_CH:(s + 1) * SSM_CH, :] = jnp.where(keep, rolled, 0.0).astype(bf16)
            e = float(SSM_T - 1 - s) if d == 0 else float(s)
            s_re, s_im = _cpow(lr_r, li_r, dt, e)
            bpr_ref[d, s * SSM_CH:(s + 1) * SSM_CH, :] = (bb_re * s_re - bb_im * s_im).astype(bf16)
            bpi_ref[d, s * SSM_CH:(s + 1) * SSM_CH, :] = (bb_re * s_im + bb_im * s_re).astype(bf16)
        t_re, t_im = _cpow(lr_r, li_r, dt, float(SSM_T))
        ltr_ref[d] = t_re
        lti_ref[d] = t_im


def _ssm_prep_call(lam_re, lam_im, log_dt, b_re, b_im, c_re, c_im):
    G, P, C = SSM_GROUPS, SSM_P, SSM_CH
    col = lambda a: a.reshape(2, G, P, 1)
    rowv = lambda a: a.reshape(2, G, 1, P)
    bt = lambda a: jnp.swapaxes(a, 2, 3)
    ct = lambda a: jnp.tile(jnp.swapaxes(a, 2, 3), (1, 1, 1, SSM_T))
    spec = lambda r, c_: pl.BlockSpec((2, None, r, c_), lambda g: (0, g, 0, 0))
    return pl.pallas_call(
        _ssm_prep_kernel,
        grid=(G,),
        in_specs=[spec(1, 1), spec(P, 1), spec(P, 1), spec(1, P), spec(1, P),
                  spec(C, P), spec(C, P), spec(P, SSM_TC), spec(P, SSM_TC)],
        out_specs=[spec(SSM_TC, SSM_TC), spec(SSM_TC, P), spec(SSM_TC, P), spec(P, SSM_TC), spec(P, SSM_TC),
                   spec(1, P), spec(1, P)],
        out_shape=[
            jax.ShapeDtypeStruct((2, G, SSM_TC, SSM_TC), bf16),
            jax.ShapeDtypeStruct((2, G, SSM_TC, P), bf16),
            jax.ShapeDtypeStruct((2, G, SSM_TC, P), bf16),
            jax.ShapeDtypeStruct((2, G, P, SSM_TC), bf16),
            jax.ShapeDtypeStruct((2, G, P, SSM_TC), bf16),
            jax.ShapeDtypeStruct((2, G, 1, P), f32),
            jax.ShapeDtypeStruct((2, G, 1, P), f32),
        ],
        name="ssm_prep",
    )(log_dt.reshape(2, G, 1, 1), col(lam_re), col(lam_im), rowv(lam_re), rowv(lam_im),
      bt(b_re), bt(b_im), ct(c_re), ct(c_im))


def _shift_rows(a, rows, down):
    m = a.shape[0]
    r = lax.broadcasted_iota(jnp.int32, a.shape, 0)
    if down:
        return jnp.where(r >= rows, pltpu.roll(a, rows, 0), 0.0)
    return jnp.where(r < m - rows, pltpu.roll(a, m - rows, 0), 0.0)


def _ssm_kernel(*refs, batch, nchunk, has_h0):
    if has_h0:
        (x_ref, km_ref, bpr_ref, bpi_ref, cpr_ref, cpi_ref, ltr_ref, lti_ref, d_ref, h0_ref, y_ref) = refs
    else:
        (x_ref, km_ref, bpr_ref, bpi_ref, cpr_ref, cpi_ref, ltr_ref, lti_ref, d_ref, y_ref, fin_ref) = refs
    m = batch * nchunk
    xf = x_ref[...]
    xb = xf.astype(bf16)
    y = d_ref[...] * xf
    row = lax.broadcasted_iota(jnp.int32, (m, SSM_P), 0)
    for d in range(2):
        down = d == 0
        y = y + _dot(xb, km_ref[d])
        e_re = _dot(xb, bpr_ref[d])
        e_im = _dot(xb, bpi_ref[d])
        a_re, a_im = ltr_ref[d], lti_ref[d]
        entry = (row < batch) if down else (row >= m - batch)
        if has_h0:
            h_re_t = jnp.zeros((m, SSM_P), f32)
            h_im_t = jnp.zeros((m, SSM_P), f32)
            for b in range(batch):
                mine = (row & (batch - 1)) == b
                h_re_t = jnp.where(mine, h0_ref[d, 0, b:b + 1, :], h_re_t)
                h_im_t = jnp.where(mine, h0_ref[d, 1, b:b + 1, :], h_im_t)
            e_re = e_re + jnp.where(entry, a_re * h_re_t - a_im * h_im_t, 0.0)
            e_im = e_im + jnp.where(entry, a_re * h_im_t + a_im * h_re_t, 0.0)
        s_re, s_im = e_re, e_im
        step = 1
        while step < nchunk:
            p_re = _shift_rows(s_re, step * batch, down)
            p_im = _shift_rows(s_im, step * batch, down)
            s_re, s_im = (s_re + a_re * p_re - a_im * p_im, s_im + a_re * p_im + a_im * p_re)
            a_re, a_im = a_re * a_re - a_im * a_im, 2.0 * a_re * a_im
            step *= 2
        hs_re = _shift_rows(s_re, batch, down)
        hs_im = _shift_rows(s_im, batch, down)
        if has_h0:
            hs_re = jnp.where(entry, h_re_t, hs_re)
            hs_im = jnp.where(entry, h_im_t, hs_im)
        else:
            if down:
                fin_ref[d, 0] = s_re[m - batch:, :]
                fin_ref[d, 1] = s_im[m - batch:, :]
            else:
                fin_ref[d, 0] = s_re[:batch, :]
                fin_ref[d, 1] = s_im[:batch, :]
        y = y + _dot(hs_re.astype(bf16), cpr_ref[d]) + _dot(hs_im.astype(bf16), cpi_ref[d])
    y_ref[...] = y


def _ssm_call(xg, mats, d_tiled, h0, batch, nchunk):
    G, P = SSM_GROUPS, SSM_P
    m = batch * nchunk
    km, bpr, bpi, cpr, cpi, ltr, lti = mats
    spec = lambda r, c_: pl.BlockSpec((2, None, r, c_), lambda g: (0, g, 0, 0))
    in_specs = [
        pl.BlockSpec((None, m, SSM_TC), lambda g: (g, 0, 0)),
        spec(SSM_TC, SSM_TC), spec(SSM_TC, P), spec(SSM_TC, P), spec(P, SSM_TC), spec(P, SSM_TC),
        spec(1, P), spec(1, P),
        pl.BlockSpec((None, 1, SSM_TC), lambda g: (g, 0, 0)),
    ]
    args = [xg, km, bpr, bpi, cpr, cpi, ltr, lti, d_tiled]
    out_specs = [pl.BlockSpec((None, m, SSM_TC), lambda g: (g, 0, 0))]
    out_shape = [jax.ShapeDtypeStruct((G, m, SSM_TC), f32)]
    st_spec = pl.BlockSpec((None, 2, 2, batch, P), lambda g: (g, 0, 0, 0, 0))
    if h0 is not None:
        in_specs.append(st_spec)
        args.append(h0)
    else:
        out_specs.append(st_spec)
        out_shape.append(jax.ShapeDtypeStruct((G, 2, 2, batch, P), f32))
    return pl.pallas_call(
        functools.partial(_ssm_kernel, batch=batch, nchunk=nchunk, has_h0=h0 is not None),
        grid=(G,),
        in_specs=in_specs,
        out_specs=out_specs,
        out_shape=out_shape,
        compiler_params=pltpu.CompilerParams(dimension_semantics=("arbitrary",)),
        name="ssm_latent" if h0 is not None else "ssm_context",
    )(*args)


def _attention(q, keys, sink_ref, masks):
    scale = HEAD_DIM ** -0.5
    rows = q.shape[0]
    lane = lax.broadcasted_iota(jnp.int32, (1, KV_W), 1)
    half = [lane < HEAD_DIM, lane >= HEAD_DIM]
    kb = []
    for k, v in keys:
        k_sw = pltpu.roll(k, HEAD_DIM, 1)
        v_sw = pltpu.roll(v, HEAD_DIM, 1)
        kb.append(((k.astype(bf16), k_sw.astype(bf16)), (v, v_sw)))
    outs = []
    for pair in range(N_HEADS // 2):
        q_pair = q[:, pair * KV_W:(pair + 1) * KV_W]
        o_pair = None
        for pp in range(2):
            hd = 2 * pair + pp
            kh = hd // Q_PER_KV
            src = 0 if kh == pp else 1
            qh = jnp.where(half[pp], q_pair, 0.0).astype(bf16)
            sink = sink_ref[hd]
            scores = []
            m = jnp.zeros((rows, 1), f32) + sink
            for (kk, _), mask in zip(kb, masks):
                s = lax.dot_general(qh, kk[src], (((1,), (1,)), ((), ())), preferred_element_type=f32) * scale
                if mask is not None:
                    s = jnp.where(mask, s, NEG_INF)
                scores.append(s)
                m = jnp.maximum(m, jnp.max(s, axis=-1, keepdims=True))
            den = jnp.exp(sink - m)
            acc = None
            for s, (_, vv) in zip(scores, kb):
                e = jnp.exp(s - m)
                den = den + jnp.sum(e, axis=-1, keepdims=True)
                o = _dot(e.astype(bf16), jnp.where(half[pp], vv[src], 0.0).astype(bf16))
                acc = o if acc is None else acc + o
            acc = acc / den
            o_pair = acc if o_pair is None else o_pair + acc
        outs.append(o_pair)
    return jnp.concatenate(outs, axis=-1)


def _p3_kernel(*refs, latent, final, seq_len):
    it = iter(refs)
    x_ref, g_ref, mod_ref, fg_ref, wq_ref, wg_ref, wm_ref = (next(it) for _ in range(7))
    if latent:
        kp_ref, kc_ref, kn_ref, vp_ref, vc_ref, vn_ref, ck_ref, cv_ref, cos_ref, sin_ref = (next(it) for _ in range(10))
    else:
        kc_ref, vc_ref = next(it), next(it)
    sink_ref = next(it)
    xcp_ref, xcc_ref, xcn_ref, xpp_ref, xpc_ref, xpn_ref, ys_ref = (next(it) for _ in range(7))
    dw_ref, db_ref, lg_ref, lb_ref, pw_ref, ps_ref, glu_ref, wbr_ref, wo_ref = (next(it) for _ in range(9))
    o_ref = next(it)
    cbuf, pbuf = next(it), next(it)

    j = pl.program_id(1)
    nt = pl.num_programs(1)
    has_prev = j > 0
    has_next = j < nt - 1
    t0 = j * TM

    x = x_ref[...]
    mod = mod_ref[...]
    hb = _modnorm(x, g_ref[...], mod).astype(bf16)

    q = _dot(hb, wq_ref[...])
    if latent:
        q = _rope(q, cos_ref[...], sin_ref[...])
        k_loc = jnp.concatenate([kp_ref[...], kc_ref[...], kn_ref[...]], axis=0)
        v_loc = jnp.concatenate([vp_ref[...], vc_ref[...], vn_ref[...]], axis=0)
        span = TM + 2 * WINDOW
        qpos = t0 + lax.broadcasted_iota(jnp.int32, (TM, span), 0)
        kpos = t0 - WINDOW + lax.broadcasted_iota(jnp.int32, (TM, span), 1)
        band = (jnp.abs(qpos - kpos) <= WINDOW) & (kpos >= 0) & (kpos < seq_len)
        o_att = _attention(q, [(ck_ref[...], cv_ref[...]), (k_loc, v_loc)], sink_ref, [None, band])
    else:
        o_att = _attention(q, [(kc_ref[...], vc_ref[...])], sink_ref, [None])

    gates = _dot(hb, wg_ref[...])

    cbuf[0:CONV_HALO, :] = jnp.where(has_prev, xcp_ref[...], 0.0)
    cbuf[CONV_HALO:CONV_HALO + TM, :] = xcc_ref[...]
    cbuf[CONV_HALO + TM:, :] = jnp.where(has_next, xcn_ref[...], 0.0)
    acc = jnp.zeros((TM, BR_W), f32) + db_ref[...]
    for tap in range(CONV_K):
        off = CONV_HALO - CONV_K // 2 + tap
        acc = acc + cbuf[off:off + TM, :] * dw_ref[tap:tap + 1, :]
    mu = jnp.mean(acc, axis=-1, keepdims=True)
    cen = acc - mu
    yn = cen * lax.rsqrt(jnp.mean(cen * cen, axis=-1, keepdims=True) + EPS)
    o_conv = _silu(yn * lg_ref[...] + lb_ref[...])

    pbuf[0:POOL_HALO, :] = jnp.where(has_prev, xpp_ref[...], 0.0)
    pbuf[POOL_HALO:POOL_HALO + TM, :] = xpc_ref[...]
    pbuf[POOL_HALO + TM:, :] = jnp.where(has_next, xpn_ref[...], 0.0)
    tpos = t0 + lax.broadcasted_iota(jnp.int32, (TM, 1), 0)
    pooled = []
    for gi, w in enumerate(POOL_WINDOWS):
        lanes = slice(gi * POOL_GW, (gi + 1) * POOL_GW)
        tot = None
        for r in range(-(w // 2), w - w // 2):
            piece = pbuf[POOL_HALO + r:POOL_HALO + r + TM, lanes]
            tot = piece if tot is None else tot + piece
        lo = jnp.maximum(tpos - w // 2, 0)
        hi = jnp.minimum(tpos + (w - w // 2), seq_len)
        cnt = (hi - lo).astype(f32)
        dlt = tot / cnt - pbuf[POOL_HALO:POOL_HALO + TM, lanes]
        pooled.append(_dot(dlt.astype(bf16), pw_ref[gi]))
    o_pool = jnp.concatenate(pooled, axis=-1) * ps_ref[...]

    z = _dot(jax.nn.gelu(ys_ref[...]).astype(bf16), glu_ref[...])
    o_ssm = z[:, :BR_W] * jax.nn.sigmoid(z[:, BR_W:])

    merged = None
    for n, o_n in enumerate((o_att, o_conv, o_pool, o_ssm)):
        br = (o_n * _silu(gates[:, n * BR_W:(n + 1) * BR_W])).astype(bf16)
        proj = _dot(br, wbr_ref[n])
        mg = jax.nn.sigmoid(_dot(hb, wm_ref[:, n * D_MODEL:(n + 1) * D_MODEL]))
        merged = mg * proj if merged is None else merged + mg * proj
    out = x + mod[:, 2 * D_MODEL:] * _dot(merged.astype(bf16), wo_ref[...])
    if final:
        out = out * lax.rsqrt(jnp.mean(out * out, axis=-1, keepdims=True) + EPS) * fg_ref[...]
    o_ref[...] = out


def _p3_call(x, norm_g, mod, final_g, wq, wg, wm, k, v, ctx_kv, rope_q, sink, xc, xp, ys, conv, pool, glu_w,
             w_br, w_out, layer, latent, final):
    B, L, _ = x.shape
    nt = L // TM
    if latent:
        mod_idx = lambda b, j: (layer * MOD_ROWS + 1 + b, 0, 0)
    else:
        mod_idx = lambda b, j: (layer * MOD_ROWS, 0, 0)
    row = lambda w: pl.BlockSpec((None, TM, w), lambda b, j: (b, j, 0))

    def halo(rows, w, side):
        per = TM // rows
        last = L // rows - 1
        if side < 0:
            return pl.BlockSpec((None, rows, w), lambda b, j: (b, jnp.maximum(j * per - 1, 0), 0))
        return pl.BlockSpec((None, rows, w), lambda b, j: (b, jnp.minimum((j + 1) * per, last), 0))

    in_specs = [row(D_MODEL), _const_spec((1, D_MODEL)), pl.BlockSpec((None, 1, 3 * D_MODEL), mod_idx),
                _const_spec((1, D_MODEL)), _const_spec(wq.shape), _const_spec(wg.shape), _const_spec(wm.shape)]
    args = [x, norm_g, mod, final_g, wq, wg, wm]
    if latent:
        ck, cv = ctx_kv
        kv_c = pl.BlockSpec((None, ck.shape[1], KV_W), lambda b, j: (b, 0, 0))
        in_specs += [halo(WINDOW, KV_W, -1), row(KV_W), halo(WINDOW, KV_W, 1),
                     halo(WINDOW, KV_W, -1), row(KV_W), halo(WINDOW, KV_W, 1), kv_c, kv_c,
                     pl.BlockSpec((TM, N_HEADS * HEAD_DIM), lambda b, j: (j, 0)),
                     pl.BlockSpec((TM, N_HEADS * HEAD_DIM), lambda b, j: (j, 0))]
        args += [k, k, k, v, v, v, ck, cv, rope_q[0], rope_q[1]]
    else:
        in_specs += [row(KV_W), row(KV_W)]
        args += [k, v]
    in_specs.append(pl.BlockSpec(memory_space=pltpu.SMEM))
    args.append(sink)
    in_specs += [halo(CONV_HALO, BR_W, -1), row(BR_W), halo(CONV_HALO, BR_W, 1),
                 halo(POOL_HALO, BR_W, -1), row(BR_W), halo(POOL_HALO, BR_W, 1), row(BR_W)]
    args += [xc, xc, xc, xp, xp, xp, ys]
    tail = list(conv) + list(pool) + [glu_w, w_br, w_out]
    in_specs += [_const_spec(a.shape) for a in tail]
    args += tail
    return pl.pallas_call(
        functools.partial(_p3_kernel, latent=latent, final=final, seq_len=L),
        grid=(B, nt),
        in_specs=in_specs,
        out_specs=row(D_MODEL),
        out_shape=jax.ShapeDtypeStruct((B, L, D_MODEL), f32),
        scratch_shapes=[pltpu.VMEM((TM + 2 * CONV_HALO, BR_W), f32), pltpu.VMEM((TM + 2 * POOL_HALO, BR_W), f32)],
        compiler_params=pltpu.CompilerParams(
            dimension_semantics=("arbitrary", "arbitrary"), vmem_limit_bytes=V7X_VMEM_LIMIT),
        name="p3_latent" if latent else "p3_context",
    )(*args)


def _to_chunks(xs):
    B, L, _ = xs.shape
    nc = L // SSM_T
    a = xs.reshape(B, nc, SSM_T, SSM_GROUPS, SSM_CH)
    return jnp.transpose(a, (3, 1, 0, 2, 4)).reshape(SSM_GROUPS, nc * B, SSM_TC)


def _from_chunks(yg, B, L):
    nc = L // SSM_T
    a = yg.reshape(SSM_GROUPS, nc, B, SSM_T, SSM_CH)
    return jnp.transpose(a, (2, 1, 3, 0, 4)).reshape(B, L, BR_W)


def _rope_tables(L, heads):
    rows = L // GRID_W
    row = jnp.repeat(jnp.arange(rows, dtype=f32), GRID_W)
    col = jnp.tile(jnp.arange(GRID_W, dtype=f32), rows)
    n_freq = HEAD_DIM // 4
    inv = ROPE_THETA ** (-jnp.arange(n_freq, dtype=f32) / n_freq)
    ang = jnp.concatenate([row[:, None] * inv, col[:, None] * inv], axis=-1)
    cos, sin = jnp.cos(ang), jnp.sin(ang)
    return (jnp.tile(jnp.concatenate([cos, cos], axis=-1), (1, heads)),
            jnp.tile(jnp.concatenate([-sin, sin], axis=-1), (1, heads)))


def _cols(w, *names):
    return jnp.concatenate([w[:, IN_OFF[n][0]:IN_OFF[n][1]] for n in names], axis=1).astype(bf16)


def kernel(x_prompt, x_sample, cache_k, cache_v, state_ssm, c, c_ctx, norm_g, w_ada, b_ada, w_in, attn_sink, conv_dw, conv_db, conv_ln_g, conv_ln_b, pool_w, pool_scale, ssm_lam_re, ssm_lam_im, ssm_log_dt, ssm_b_re, ssm_b_im, ssm_c_re, ssm_c_im, ssm_d, ssm_glu_w, w_br, w_out, final_g):
    BP, LP, _ = x_prompt.shape
    BS, LS, _ = x_sample.shape
    PAST = cache_k.shape[2]
    c_rows = jnp.concatenate([c_ctx[None, :], c, jnp.zeros((MOD_ROWS - 1 - BS, D_MODEL), f32)], axis=0)
    mod = _mod_call(c_rows, w_ada, b_ada).reshape(DEPTH * MOD_ROWS, 1, 3 * D_MODEL)
    rope_k = _rope_tables(LS, N_KV)
    rope_q = _rope_tables(LS, N_HEADS)
    fg = final_g.reshape(1, D_MODEL)

    xp, xs = x_prompt, x_sample
    new_k, new_v, new_st = [], [], []
    for l in range(DEPTH):
        ng = norm_g[l].reshape(1, D_MODEL)
        w1 = _cols(w_in[l], "k", "v", "a_conv", "x_pool", "x_ssm")
        wq = _cols(w_in[l], "q")
        wg = _cols(w_in[l], "g_att", "g_conv", "g_pool", "g_ssm")
        wm = _cols(w_in[l], "g_mrg")
        mats = _ssm_prep_call(ssm_lam_re[l], ssm_lam_im[l], ssm_log_dt[l], ssm_b_re[l], ssm_b_im[l],
                              ssm_c_re[l], ssm_c_im[l])
        d_tiled = jnp.tile(ssm_d[l].reshape(SSM_GROUPS, 1, SSM_CH), (1, 1, SSM_T))
        dw = jnp.concatenate([conv_dw[l], jnp.zeros((1, BR_W), f32)], axis=0)
        conv = (dw, conv_db[l].reshape(1, BR_W), conv_ln_g[l].reshape(1, BR_W), conv_ln_b[l].reshape(1, BR_W))
        pool = (pool_w[l].astype(bf16), pool_scale[l].reshape(1, BR_W))
        glu = ssm_glu_w[l].astype(bf16)
        wbr = w_br[l].astype(bf16)
        wo = w_out[l].astype(bf16)
        final = l == DEPTH - 1

        k, v, xc, xpl, xss = _p1_call(xp, ng, mod, w1, l, False, None)
        yg, fin = _ssm_call(_to_chunks(xss), mats, d_tiled, None, BP, LP // SSM_T)
        ys = _from_chunks(yg, BP, LP)
        xp = _p3_call(xp, ng, mod, fg, wq, wg, wm, k, v, None, None, attn_sink[l], xc, xpl, ys, conv, pool,
                      glu, wbr, wo, l, False, final)
        new_k.append(k.reshape(BP, LP, N_KV, HEAD_DIM))
        new_v.append(v.reshape(BP, LP, N_KV, HEAD_DIM))
        new_st.append(jnp.transpose(fin, (3, 1, 2, 0, 4)))

        k, v, xc, xpl, xss = _p1_call(xs, ng, mod, w1, l, True, rope_k)
        h0 = jnp.transpose(state_ssm[:, l], (3, 1, 2, 0, 4))
        (yg,) = _ssm_call(_to_chunks(xss), mats, d_tiled, h0, BS, LS // SSM_T)
        ys = _from_chunks(yg, BS, LS)
        ctx = (cache_k[:, l].reshape(BS, PAST, KV_W), cache_v[:, l].reshape(BS, PAST, KV_W))
        xs = _p3_call(xs, ng, mod, fg, wq, wg, wm, k, v, ctx, rope_q, attn_sink[l], xc, xpl, ys, conv, pool,
                      glu, wbr, wo, l, True, final)

    return (xp, xs, jnp.stack(new_k, axis=1), jnp.stack(new_v, axis=1), jnp.stack(new_st, axis=1))
```

```python
import functools
import math

import jax
import jax.numpy as jnp
from jax import lax
from jax.experimental import pallas as pl
from jax.experimental.pallas import tpu as pltpu

f32 = jnp.float32
bf16 = jnp.bfloat16
HIGHEST = lax.Precision.HIGHEST

D_MODEL = 1024
DEPTH = 2
GRID_W = 64
BR_W = 512
N_BRANCH = 4
N_HEADS = 8
N_KV = 2
HEAD_DIM = 64
Q_PER_KV = N_HEADS // N_KV
KV_W = N_KV * HEAD_DIM
WINDOW = 128
ROPE_THETA = 10000.0
NEG_INF = -1e30
CONV_K = 31
CONV_HALO = 16
POOL_WINDOWS = (2, 4, 8, 16)
POOL_GROUPS = 4
POOL_GW = BR_W // POOL_GROUPS
POOL_HALO = 8
SSM_CH = 16
SSM_GROUPS = BR_W // SSM_CH
SSM_P = 64
SSM_T = 16
SSM_TC = SSM_T * SSM_CH
EPS = 1e-6

IN_SIZES = (N_HEADS * HEAD_DIM, KV_W, KV_W, BR_W, 2 * BR_W, BR_W, BR_W, BR_W, BR_W, BR_W, N_BRANCH * D_MODEL)
IN_NAMES = ("q", "k", "v", "g_att", "a_conv", "g_conv", "x_pool", "g_pool", "x_ssm", "g_ssm", "g_mrg")
IN_OFF = {}
_o = 0
for _n, _s in zip(IN_NAMES, IN_SIZES):
    IN_OFF[_n] = (_o, _o + _s)
    _o += _s

TM = 256
MOD_ROWS = 8
V7X_VMEM_LIMIT = 56 * 1024 * 1024


def _silu(x):
    return x * jax.nn.sigmoid(x)


def _modnorm(x, g, mod):
    shift = mod[:, :D_MODEL]
    scale = mod[:, D_MODEL:2 * D_MODEL]
    xn = x * lax.rsqrt(jnp.mean(x * x, axis=-1, keepdims=True) + EPS)
    return (xn * g) * (1.0 + scale) + shift


def _dot(a, b):
    return jnp.dot(a, b, preferred_element_type=f32)


def _const_spec(shape):
    nd = len(shape)
    return pl.BlockSpec(shape, lambda *_: (0,) * nd, pipeline_mode=pl.Buffered(1))


def _mod_kernel(c_ref, w_ref, b_ref, o_ref):
    s = _silu(c_ref[...])
    o_ref[0] = jnp.dot(s, w_ref[0], precision=HIGHEST, preferred_element_type=f32) + b_ref[0]


def _mod_call(c_rows, w_ada, b_ada):
    nb = 512
    return pl.pallas_call(
        _mod_kernel,
        grid=(DEPTH, 3 * D_MODEL // nb),
        in_specs=[
            pl.BlockSpec((MOD_ROWS, D_MODEL), lambda l, j: (0, 0)),
            pl.BlockSpec((1, D_MODEL, nb), lambda l, j: (l, 0, j)),
            pl.BlockSpec((1, 1, nb), lambda l, j: (l, 0, j)),
        ],
        out_specs=pl.BlockSpec((1, MOD_ROWS, nb), lambda l, j: (l, 0, j)),
        out_shape=jax.ShapeDtypeStruct((DEPTH, MOD_ROWS, 3 * D_MODEL), f32),
        name="mod",
    )(c_rows, w_ada, b_ada.reshape(DEPTH, 1, 3 * D_MODEL))


def _rope(x, cos, sin_signed):
    width = x.shape[-1]
    lane = lax.broadcasted_iota(jnp.int32, x.shape, 1)
    first_half = (lane & (HEAD_DIM - 1)) < HEAD_DIM // 2
    from_up = pltpu.roll(x, width - HEAD_DIM // 2, 1)
    from_dn = pltpu.roll(x, HEAD_DIM // 2, 1)
    return x * cos + jnp.where(first_half, from_up, from_dn) * sin_signed


def _p1_kernel(*refs, rope):
    if rope:
        x_ref, g_ref, mod_ref, w_ref, cos_ref, sin_ref, k_ref, v_ref, xc_ref, xp_ref, xs_ref = refs
    else:
        x_ref, g_ref, mod_ref, w_ref, k_ref, v_ref, xc_ref, xp_ref, xs_ref = refs
    h = _modnorm(x_ref[...], g_ref[...], mod_ref[...]).astype(bf16)
    kv = _dot(h, w_ref[:, 0:2 * KV_W])
    k = kv[:, :KV_W]
    if rope:
        k = _rope(k, cos_ref[...], sin_ref[...])
    k_ref[...] = k
    v_ref[...] = kv[:, KV_W:]
    o = 2 * KV_W
    a = _dot(h, w_ref[:, o:o + 2 * BR_W])
    xc_ref[...] = a[:, :BR_W] * jax.nn.sigmoid(a[:, BR_W:])
    o += 2 * BR_W
    xp_ref[...] = _dot(h, w_ref[:, o:o + BR_W])
    o += BR_W
    xs_ref[...] = _dot(h, w_ref[:, o:o + BR_W])


def _p1_call(x, norm_g, mod, w1, layer, latent, rope_k):
    B, L, _ = x.shape
    nt = L // TM
    if latent:
        mod_idx = lambda b, j: (layer * MOD_ROWS + 1 + b, 0, 0)
    else:
        mod_idx = lambda b, j: (layer * MOD_ROWS, 0, 0)
    row = lambda w: pl.BlockSpec((None, TM, w), lambda b, j: (b, j, 0))
    in_specs = [
        row(D_MODEL),
        _const_spec((1, D_MODEL)),
        pl.BlockSpec((None, 1, 3 * D_MODEL), mod_idx),
        _const_spec(w1.shape),
    ]
    args = [x, norm_g, mod, w1]
    if latent:
        in_specs += [pl.BlockSpec((TM, KV_W), lambda b, j: (j, 0))] * 2
        args += list(rope_k)
    shp = lambda w: jax.ShapeDtypeStruct((B, L, w), f32)
    return pl.pallas_call(
        functools.partial(_p1_kernel, rope=latent),
        grid=(B, nt),
        in_specs=in_specs,
        out_specs=[row(KV_W), row(KV_W), row(BR_W), row(BR_W), row(BR_W)],
        out_shape=[shp(KV_W), shp(KV_W), shp(BR_W), shp(BR_W), shp(BR_W)],
        compiler_params=pltpu.CompilerParams(
            dimension_semantics=("arbitrary", "arbitrary"), vmem_limit_bytes=V7X_VMEM_LIMIT),
        name="p1_latent" if latent else "p1_context",
    )(*args)


def _cpowers(lr, li, dt, n):
    mag = jnp.exp(lr * dt)
    a_re, a_im = mag * jnp.cos(li * dt), mag * jnp.sin(li * dt)
    out = [(jnp.ones_like(a_re), jnp.zeros_like(a_im))]
    for _ in range(n):
        p_re, p_im = out[-1]
        out.append((p_re * a_re - p_im * a_im, p_re * a_im + p_im * a_re))
    return out


def _ssm_prep_kernel(ldt_ref, lrc_ref, lic_ref, lrr_ref, lir_ref, btr_ref, bti_ref, ctr_ref, cti_ref,
                     km_ref, bpr_ref, bpi_ref, cpr_ref, cpi_ref, ltr_ref, lti_ref):
    lane = lax.broadcasted_iota(jnp.int32, (1, SSM_TC), 1)
    lane_row = lax.broadcasted_iota(jnp.int32, (SSM_CH, SSM_TC), 1)
    blk = jnp.right_shift(lane, int(math.log2(SSM_CH)))
    for d in range(2):
        dt = jnp.exp(ldt_ref[d])
        lr_r, li_r = lrr_ref[d], lir_ref[d]
        pw_r = _cpowers(lr_r, li_r, dt, SSM_T)
        pw_c = _cpowers(lrc_ref[d], lic_ref[d], dt, SSM_T + 1)
        a_re, a_im = pw_r[1]
        den = lr_r * lr_r + li_r * li_r
        w_re = ((a_re - 1.0) * lr_r + a_im * li_r) / den
        w_im = (a_im * lr_r - (a_re - 1.0) * li_r) / den
        bt_re, bt_im = btr_ref[d], bti_ref[d]
        bb_re = w_re * bt_re - w_im * bt_im
        bb_im = w_re * bt_im + w_im * bt_re
        ct_re, ct_im = ctr_ref[d], cti_ref[d]
        p_re = jnp.zeros((SSM_P, SSM_TC), f32)
        p_im = jnp.zeros((SSM_P, SSM_TC), f32)
        q_re = jnp.zeros((SSM_P, SSM_TC), f32)
        q_im = jnp.zeros((SSM_P, SSM_TC), f32)
        for j in range(SSM_T):
            tau = j if d == 0 else SSM_T - 1 - j
            here = blk == j
            p_re = jnp.where(here, pw_c[tau][0], p_re)
            p_im = jnp.where(here, pw_c[tau][1], p_im)
            q_re = jnp.where(here, pw_c[tau + 1][0], q_re)
            q_im = jnp.where(here, pw_c[tau + 1][1], q_im)
        w_re2 = p_re * ct_re - p_im * ct_im
        w_im2 = p_re * ct_im + p_im * ct_re
        grow = (jnp.dot(bb_re, w_re2, precision=HIGHEST, preferred_element_type=f32)
                - jnp.dot(bb_im, w_im2, precision=HIGHEST, preferred_element_type=f32))
        cpr_ref[d] = (q_re * ct_re - q_im * ct_im).astype(bf16)
        cpi_ref[d] = (-(q_re * ct_im + q_im * ct_re)).astype(bf16)
        for s in range(SSM_T):
            if d == 0:
                shift = SSM_CH * s
                keep = lane_row >= shift
            else:
                shift = (SSM_CH * (s + 1)) % SSM_TC
                keep = lane_row < SSM_CH * (s + 1)
            rolled = pltpu.roll(grow, shift, 1) if shift else grow
            km_ref[d, s * SSM_CH:(s + 1) * SSM_CH, :] = jnp.where(keep, rolled, 0.0).astype(bf16)
            s_re, s_im = pw_r[SSM_T - 1 - s] if d == 0 else pw_r[s]
            bpr_ref[d, s * SSM_CH:(s + 1) * SSM_CH, :] = (bb_re * s_re - bb_im * s_im).astype(bf16)
            bpi_ref[d, s * SSM_CH:(s + 1) * SSM_CH, :] = (bb_re * s_im + bb_im * s_re).astype(bf16)
        ltr_ref[d], lti_ref[d] = pw_r[SSM_T]


def _ssm_prep_call(lam_re, lam_im, log_dt, b_re, b_im, c_re, c_im):
    G, P, C = SSM_GROUPS, SSM_P, SSM_CH
    col = lambda a: a.reshape(2, G, P, 1)
    rowv = lambda a: a.reshape(2, G, 1, P)
    bt = lambda a: jnp.swapaxes(a, 2, 3)
    ct = lambda a: jnp.tile(jnp.swapaxes(a, 2, 3), (1, 1, 1, SSM_T))
    spec = lambda r, c_: pl.BlockSpec((2, None, r, c_), lambda g: (0, g, 0, 0))
    return pl.pallas_call(
        _ssm_prep_kernel,
        grid=(G,),
        in_specs=[spec(1, 1), spec(P, 1), spec(P, 1), spec(1, P), spec(1, P),
                  spec(C, P), spec(C, P), spec(P, SSM_TC), spec(P, SSM_TC)],
        out_specs=[spec(SSM_TC, SSM_TC), spec(SSM_TC, P), spec(SSM_TC, P), spec(P, SSM_TC), spec(P, SSM_TC),
                   spec(1, P), spec(1, P)],
        out_shape=[
            jax.ShapeDtypeStruct((2, G, SSM_TC, SSM_TC), bf16),
            jax.ShapeDtypeStruct((2, G, SSM_TC, P), bf16),
            jax.ShapeDtypeStruct((2, G, SSM_TC, P), bf16),
            jax.ShapeDtypeStruct((2, G, P, SSM_TC), bf16),
            jax.ShapeDtypeStruct((2, G, P, SSM_TC), bf16),
            jax.ShapeDtypeStruct((2, G, 1, P), f32),
            jax.ShapeDtypeStruct((2, G, 1, P), f32),
        ],
        name="ssm_prep",
    )(log_dt.reshape(2, G, 1, 1), col(lam_re), col(lam_im), rowv(lam_re), rowv(lam_im),
      bt(b_re), bt(b_im), ct(c_re), ct(c_im))


LANES = 128
GPT = LANES // SSM_CH


def _shift_chunks(a, k, chunk_id, nchunk, down):
    m = a.shape[0]
    if down:
        return jnp.where(chunk_id >= k, pltpu.roll(a, k, 0), 0.0)
    return jnp.where(chunk_id < nchunk - k, pltpu.roll(a, m - k, 0), 0.0)


def _ssm_kernel(*refs, batch, nchunk, has_h0):
    if has_h0:
        (x_ref, km_ref, bpr_ref, bpi_ref, cpr_ref, cpi_ref, ltr_ref, lti_ref, d_ref, h0_ref, y_ref, ybuf) = refs
    else:
        (x_ref, km_ref, bpr_ref, bpi_ref, cpr_ref, cpi_ref, ltr_ref, lti_ref, d_ref, y_ref, fin_ref,
         ybuf, sbuf) = refs
    gg = pl.program_id(1)
    m = batch * nchunk
    slot = jnp.right_shift(lax.broadcasted_iota(jnp.int32, (1, LANES), 1), int(math.log2(SSM_CH)))

    tiles = []
    for hh in range(SSM_T // GPT):
        acc = jnp.zeros((m, LANES), f32)
        for ss in range(GPT):
            a_s = x_ref[pl.ds(hh * GPT + ss, m, stride=SSM_T), :]
            r_s = pltpu.roll(a_s, ss * SSM_CH, 1) if ss else a_s
            acc = jnp.where(slot == ((gg + ss) & (GPT - 1)), r_s, acc)
        tiles.append(pltpu.roll(acc, (LANES - gg * SSM_CH) & (LANES - 1), 1))
    xb = jnp.concatenate(tiles, axis=1).astype(bf16)

    row = lax.broadcasted_iota(jnp.int32, (m, SSM_P), 0)
    chunk_id = row & (nchunk - 1)
    y = None
    for d in range(2):
        down = d == 0
        yk = _dot(xb, km_ref[d])
        y = yk if y is None else y + yk
        e_re = _dot(xb, bpr_ref[d])
        e_im = _dot(xb, bpi_ref[d])
        a_re, a_im = ltr_ref[d], lti_ref[d]
        entry = chunk_id == (0 if down else nchunk - 1)
        if has_h0:
            h_re_t = jnp.zeros((m, SSM_P), f32)
            h_im_t = jnp.zeros((m, SSM_P), f32)
            for b in range(batch):
                mine = jnp.right_shift(row, int(math.log2(nchunk))) == b
                h_re_t = jnp.where(mine, h0_ref[d, 0, b:b + 1, :], h_re_t)
                h_im_t = jnp.where(mine, h0_ref[d, 1, b:b + 1, :], h_im_t)
            e_re = e_re + jnp.where(entry, a_re * h_re_t - a_im * h_im_t, 0.0)
            e_im = e_im + jnp.where(entry, a_re * h_im_t + a_im * h_re_t, 0.0)
        s_re, s_im = e_re, e_im
        step = 1
        while step < nchunk:
            p_re = _shift_chunks(s_re, step, chunk_id, nchunk, down)
            p_im = _shift_chunks(s_im, step, chunk_id, nchunk, down)
            s_re, s_im = (s_re + a_re * p_re - a_im * p_im, s_im + a_re * p_im + a_im * p_re)
            a_re, a_im = a_re * a_re - a_im * a_im, 2.0 * a_re * a_im
            step *= 2
        hs_re = _shift_chunks(s_re, 1, chunk_id, nchunk, down)
        hs_im = _shift_chunks(s_im, 1, chunk_id, nchunk, down)
        if has_h0:
            hs_re = jnp.where(entry, h_re_t, hs_re)
            hs_im = jnp.where(entry, h_im_t, hs_im)
        else:
            sbuf[0] = s_re
            sbuf[1] = s_im
            last = nchunk - 1 if down else 0
            fin_ref[d, 0] = sbuf[0, pl.ds(last, batch, stride=nchunk), :]
            fin_ref[d, 1] = sbuf[1, pl.ds(last, batch, stride=nchunk), :]
        y = y + _dot(hs_re.astype(bf16), cpr_ref[d]) + _dot(hs_im.astype(bf16), cpi_ref[d])

    up = gg * SSM_CH
    ybuf[gg] = jnp.concatenate([pltpu.roll(y[:, :LANES], up, 1), pltpu.roll(y[:, LANES:], up, 1)], axis=1)

    @pl.when(gg == GPT - 1)
    def _():
        for hh in range(SSM_T // GPT):
            for tt in range(GPT):
                t = hh * GPT + tt
                acc = jnp.zeros((m, LANES), f32)
                for g2 in range(GPT):
                    acc = jnp.where(slot == ((g2 + tt) & (GPT - 1)), ybuf[g2, :, hh * LANES:(hh + 1) * LANES], acc)
                yt = pltpu.roll(acc, LANES - tt * SSM_CH, 1) if tt else acc
                a_t = x_ref[pl.ds(t, m, stride=SSM_T), :]
                y_ref[pl.ds(t, m, stride=SSM_T), :] = yt + d_ref[...] * a_t


def _ssm_call(x2d, mats, d_row, h0, batch, nchunk):
    G, P = SSM_GROUPS, SSM_P
    m = batch * nchunk
    rows = m * SSM_T
    km, bpr, bpi, cpr, cpi, ltr, lti = mats
    spec = lambda r, c_: pl.BlockSpec((2, None, r, c_), lambda q, gg: (0, q * GPT + gg, 0, 0))
    tile = pl.BlockSpec((rows, LANES), lambda q, gg: (0, q))
    in_specs = [
        tile,
        spec(SSM_TC, SSM_TC), spec(SSM_TC, P), spec(SSM_TC, P), spec(P, SSM_TC), spec(P, SSM_TC),
        spec(1, P), spec(1, P),
        pl.BlockSpec((1, LANES), lambda q, gg: (0, q)),
    ]
    args = [x2d, km, bpr, bpi, cpr, cpi, ltr, lti, d_row]
    out_specs = [tile]
    out_shape = [jax.ShapeDtypeStruct((rows, G * SSM_CH), f32)]
    scratch = [pltpu.VMEM((GPT, m, SSM_TC), f32)]
    st_spec = pl.BlockSpec((None, 2, 2, batch, P), lambda q, gg: (q * GPT + gg, 0, 0, 0, 0))
    if h0 is not None:
        in_specs.append(st_spec)
        args.append(h0)
    else:
        out_specs.append(st_spec)
        out_shape.append(jax.ShapeDtypeStruct((G, 2, 2, batch, P), f32))
        scratch.append(pltpu.VMEM((2, m, P), f32))
    return pl.pallas_call(
        functools.partial(_ssm_kernel, batch=batch, nchunk=nchunk, has_h0=h0 is not None),
        grid=(G // GPT, GPT),
        in_specs=in_specs,
        out_specs=out_specs,
        out_shape=out_shape,
        scratch_shapes=scratch,
        compiler_params=pltpu.CompilerParams(dimension_semantics=("arbitrary", "arbitrary")),
        name="ssm_latent" if h0 is not None else "ssm_context",
    )(*args)


def _attention(q, keys, sink_ref, masks):
    scale = HEAD_DIM ** -0.5
    rows = q.shape[0]
    lane = lax.broadcasted_iota(jnp.int32, (1, KV_W), 1)
    half = [lane < HEAD_DIM, lane >= HEAD_DIM]
    kb = []
    for k, v in keys:
        k_sw = pltpu.roll(k, HEAD_DIM, 1)
        v_sw = pltpu.roll(v, HEAD_DIM, 1)
        kb.append(((k.astype(bf16), k_sw.astype(bf16)), (v, v_sw)))
    outs = []
    for pair in range(N_HEADS // 2):
        q_pair = q[:, pair * KV_W:(pair + 1) * KV_W]
        o_pair = None
        for pp in range(2):
            hd = 2 * pair + pp
            kh = hd // Q_PER_KV
            src = 0 if kh == pp else 1
            qh = jnp.where(half[pp], q_pair, 0.0).astype(bf16)
            sink = sink_ref[hd]
            scores = []
            m = jnp.zeros((rows, 1), f32) + sink
            for (kk, _), mask in zip(kb, masks):
                s = lax.dot_general(qh, kk[src], (((1,), (1,)), ((), ())), preferred_element_type=f32) * scale
                if mask is not None:
                    s = jnp.where(mask, s, NEG_INF)
                scores.append(s)
                m = jnp.maximum(m, jnp.max(s, axis=-1, keepdims=True))
            den = jnp.exp(sink - m)
            acc = None
            for s, (_, vv) in zip(scores, kb):
                e = jnp.exp(s - m)
                den = den + jnp.sum(e, axis=-1, keepdims=True)
                o = _dot(e.astype(bf16), jnp.where(half[pp], vv[src], 0.0).astype(bf16))
                acc = o if acc is None else acc + o
            acc = acc / den
            o_pair = acc if o_pair is None else o_pair + acc
        outs.append(o_pair)
    return jnp.concatenate(outs, axis=-1)


def _p3_kernel(*refs, latent, final, seq_len):
    it = iter(refs)
    x_ref, g_ref, mod_ref, fg_ref, wq_ref, wg_ref, wm_ref = (next(it) for _ in range(7))
    if latent:
        kp_ref, kc_ref, kn_ref, vp_ref, vc_ref, vn_ref, ck_ref, cv_ref, cos_ref, sin_ref = (next(it) for _ in range(10))
    else:
        kc_ref, vc_ref = next(it), next(it)
    sink_ref = next(it)
    xcp_ref, xcc_ref, xcn_ref, xpp_ref, xpc_ref, xpn_ref, ys_ref = (next(it) for _ in range(7))
    dw_ref, db_ref, lg_ref, lb_ref, pw_ref, ps_ref, glu_ref, wbr_ref, wo_ref = (next(it) for _ in range(9))
    o_ref = next(it)
    cbuf, pbuf = next(it), next(it)

    j = pl.program_id(1)
    nt = pl.num_programs(1)
    has_prev = j > 0
    has_next = j < nt - 1
    t0 = j * TM

    x = x_ref[...]
    mod = mod_ref[...]
    hb = _modnorm(x, g_ref[...], mod).astype(bf16)

    q = _dot(hb, wq_ref[...])
    if latent:
        q = _rope(q, cos_ref[...], sin_ref[...])
        k_loc = jnp.concatenate([kp_ref[...], kc_ref[...], kn_ref[...]], axis=0)
        v_loc = jnp.concatenate([vp_ref[...], vc_ref[...], vn_ref[...]], axis=0)
        span = TM + 2 * WINDOW
        qpos = t0 + lax.broadcasted_iota(jnp.int32, (TM, span), 0)
        kpos = t0 - WINDOW + lax.broadcasted_iota(jnp.int32, (TM, span), 1)
        band = (jnp.abs(qpos - kpos) <= WINDOW) & (kpos >= 0) & (kpos < seq_len)
        o_att = _attention(q, [(ck_ref[...], cv_ref[...]), (k_loc, v_loc)], sink_ref, [None, band])
    else:
        o_att = _attention(q, [(kc_ref[...], vc_ref[...])], sink_ref, [None])

    gates = _dot(hb, wg_ref[...])

    cbuf[0:CONV_HALO, :] = jnp.where(has_prev, xcp_ref[...], 0.0)
    cbuf[CONV_HALO:CONV_HALO + TM, :] = xcc_ref[...]
    cbuf[CONV_HALO + TM:, :] = jnp.where(has_next, xcn_ref[...], 0.0)
    acc = jnp.zeros((TM, BR_W), f32) + db_ref[...]
    for tap in range(CONV_K):
        off = CONV_HALO - CONV_K // 2 + tap
        acc = acc + cbuf[off:off + TM, :] * dw_ref[tap:tap + 1, :]
    mu = jnp.mean(acc, axis=-1, keepdims=True)
    cen = acc - mu
    yn = cen * lax.rsqrt(jnp.mean(cen * cen, axis=-1, keepdims=True) + EPS)
    o_conv = _silu(yn * lg_ref[...] + lb_ref[...])

    pbuf[0:POOL_HALO, :] = jnp.where(has_prev, xpp_ref[...], 0.0)
    pbuf[POOL_HALO:POOL_HALO + TM, :] = xpc_ref[...]
    pbuf[POOL_HALO + TM:, :] = jnp.where(has_next, xpn_ref[...], 0.0)
    tpos = t0 + lax.broadcasted_iota(jnp.int32, (TM, 1), 0)
    pooled = []
    for gi, w in enumerate(POOL_WINDOWS):
        lanes = slice(gi * POOL_GW, (gi + 1) * POOL_GW)
        tot = None
        for r in range(-(w // 2), w - w // 2):
            piece = pbuf[POOL_HALO + r:POOL_HALO + r + TM, lanes]
            tot = piece if tot is None else tot + piece
        lo = jnp.maximum(tpos - w // 2, 0)
        hi = jnp.minimum(tpos + (w - w // 2), seq_len)
        cnt = (hi - lo).astype(f32)
        dlt = tot / cnt - pbuf[POOL_HALO:POOL_HALO + TM, lanes]
        pooled.append(_dot(dlt.astype(bf16), pw_ref[gi]))
    o_pool = jnp.concatenate(pooled, axis=-1) * ps_ref[...]

    z = _dot(jax.nn.gelu(ys_ref[...]).astype(bf16), glu_ref[...])
    o_ssm = z[:, :BR_W] * jax.nn.sigmoid(z[:, BR_W:])

    merged = None
    for n, o_n in enumerate((o_att, o_conv, o_pool, o_ssm)):
        br = (o_n * _silu(gates[:, n * BR_W:(n + 1) * BR_W])).astype(bf16)
        proj = _dot(br, wbr_ref[n])
        mg = jax.nn.sigmoid(_dot(hb, wm_ref[:, n * D_MODEL:(n + 1) * D_MODEL]))
        merged = mg * proj if merged is None else merged + mg * proj
    out = x + mod[:, 2 * D_MODEL:] * _dot(merged.astype(bf16), wo_ref[...])
    if final:
        out = out * lax.rsqrt(jnp.mean(out * out, axis=-1, keepdims=True) + EPS) * fg_ref[...]
    o_ref[...] = out


def _p3_call(x, norm_g, mod, final_g, wq, wg, wm, k, v, ctx_kv, rope_q, sink, xc, xp, ys, conv, pool, glu_w,
             w_br, w_out, layer, latent, final):
    B, L, _ = x.shape
    nt = L // TM
    if latent:
        mod_idx = lambda b, j: (layer * MOD_ROWS + 1 + b, 0, 0)
    else:
        mod_idx = lambda b, j: (layer * MOD_ROWS, 0, 0)
    row = lambda w: pl.BlockSpec((None, TM, w), lambda b, j: (b, j, 0))

    def halo(rows, w, side):
        per = TM // rows
        last = L // rows - 1
        if side < 0:
            return pl.BlockSpec((None, rows, w), lambda b, j: (b, jnp.maximum(j * per - 1, 0), 0))
        return pl.BlockSpec((None, rows, w), lambda b, j: (b, jnp.minimum((j + 1) * per, last), 0))

    in_specs = [row(D_MODEL), _const_spec((1, D_MODEL)), pl.BlockSpec((None, 1, 3 * D_MODEL), mod_idx),
                _const_spec((1, D_MODEL)), _const_spec(wq.shape), _const_spec(wg.shape), _const_spec(wm.shape)]
    args = [x, norm_g, mod, final_g, wq, wg, wm]
    if latent:
        ck, cv = ctx_kv
        kv_c = pl.BlockSpec((None, ck.shape[1], KV_W), lambda b, j: (b, 0, 0))
        in_specs += [halo(WINDOW, KV_W, -1), row(KV_W), halo(WINDOW, KV_W, 1),
                     halo(WINDOW, KV_W, -1), row(KV_W), halo(WINDOW, KV_W, 1), kv_c, kv_c,
                     pl.BlockSpec((TM, N_HEADS * HEAD_DIM), lambda b, j: (j, 0)),
                     pl.BlockSpec((TM, N_HEADS * HEAD_DIM), lambda b, j: (j, 0))]
        args += [k, k, k, v, v, v, ck, cv, rope_q[0], rope_q[1]]
    else:
        in_specs += [row(KV_W), row(KV_W)]
        args += [k, v]
    in_specs.append(pl.BlockSpec(memory_space=pltpu.SMEM))
    args.append(sink)
    in_specs += [halo(CONV_HALO, BR_W, -1), row(BR_W), halo(CONV_HALO, BR_W, 1),
                 halo(POOL_HALO, BR_W, -1), row(BR_W), halo(POOL_HALO, BR_W, 1), row(BR_W)]
    args += [xc, xc, xc, xp, xp, xp, ys]
    tail = list(conv) + list(pool) + [glu_w, w_br, w_out]
    in_specs += [_const_spec(a.shape) for a in tail]
    args += tail
    return pl.pallas_call(
        functools.partial(_p3_kernel, latent=latent, final=final, seq_len=L),
        grid=(B, nt),
        in_specs=in_specs,
        out_specs=row(D_MODEL),
        out_shape=jax.ShapeDtypeStruct((B, L, D_MODEL), f32),
        scratch_shapes=[pltpu.VMEM((TM + 2 * CONV_HALO, BR_W), f32), pltpu.VMEM((TM + 2 * POOL_HALO, BR_W), f32)],
        compiler_params=pltpu.CompilerParams(
            dimension_semantics=("arbitrary", "arbitrary"), vmem_limit_bytes=V7X_VMEM_LIMIT),
        name="p3_latent" if latent else "p3_context",
    )(*args)


def _rope_tables(L, heads):
    rows = L // GRID_W
    row = jnp.repeat(jnp.arange(rows, dtype=f32), GRID_W)
    col = jnp.tile(jnp.arange(GRID_W, dtype=f32), rows)
    n_freq = HEAD_DIM // 4
    inv = ROPE_THETA ** (-jnp.arange(n_freq, dtype=f32) / n_freq)
    ang = jnp.concatenate([row[:, None] * inv, col[:, None] * inv], axis=-1)
    cos, sin = jnp.cos(ang), jnp.sin(ang)
    return (jnp.tile(jnp.concatenate([cos, cos], axis=-1), (1, heads)),
            jnp.tile(jnp.concatenate([-sin, sin], axis=-1), (1, heads)))


def _cols(w, *names):
    return jnp.concatenate([w[:, IN_OFF[n][0]:IN_OFF[n][1]] for n in names], axis=1).astype(bf16)


def kernel(x_prompt, x_sample, cache_k, cache_v, state_ssm, c, c_ctx, norm_g, w_ada, b_ada, w_in, attn_sink, conv_dw, conv_db, conv_ln_g, conv_ln_b, pool_w, pool_scale, ssm_lam_re, ssm_lam_im, ssm_log_dt, ssm_b_re, ssm_b_im, ssm_c_re, ssm_c_im, ssm_d, ssm_glu_w, w_br, w_out, final_g):
    BP, LP, _ = x_prompt.shape
    BS, LS, _ = x_sample.shape
    PAST = cache_k.shape[2]
    c_rows = jnp.concatenate([c_ctx[None, :], c, jnp.zeros((MOD_ROWS - 1 - BS, D_MODEL), f32)], axis=0)
    mod = _mod_call(c_rows, w_ada, b_ada).reshape(DEPTH * MOD_ROWS, 1, 3 * D_MODEL)
    rope_k = _rope_tables(LS, N_KV)
    rope_q = _rope_tables(LS, N_HEADS)
    fg = final_g.reshape(1, D_MODEL)

    xp, xs = x_prompt, x_sample
    new_k, new_v, new_st = [], [], []
    for l in range(DEPTH):
        ng = norm_g[l].reshape(1, D_MODEL)
        w1 = _cols(w_in[l], "k", "v", "a_conv", "x_pool", "x_ssm")
        wq = _cols(w_in[l], "q")
        wg = _cols(w_in[l], "g_att", "g_conv", "g_pool", "g_ssm")
        wm = _cols(w_in[l], "g_mrg")
        mats = _ssm_prep_call(ssm_lam_re[l], ssm_lam_im[l], ssm_log_dt[l], ssm_b_re[l], ssm_b_im[l],
                              ssm_c_re[l], ssm_c_im[l])
        d_row = ssm_d[l].reshape(1, BR_W)
        dw = jnp.concatenate([conv_dw[l], jnp.zeros((1, BR_W), f32)], axis=0)
        conv = (dw, conv_db[l].reshape(1, BR_W), conv_ln_g[l].reshape(1, BR_W), conv_ln_b[l].reshape(1, BR_W))
        pool = (pool_w[l].astype(bf16), pool_scale[l].reshape(1, BR_W))
        glu = ssm_glu_w[l].astype(bf16)
        wbr = w_br[l].astype(bf16)
        wo = w_out[l].astype(bf16)
        final = l == DEPTH - 1

        k, v, xc, xpl, xss = _p1_call(xp, ng, mod, w1, l, False, None)
        yg, fin = _ssm_call(xss.reshape(BP * LP, BR_W), mats, d_row, None, BP, LP // SSM_T)
        ys = yg.reshape(BP, LP, BR_W)
        xp = _p3_call(xp, ng, mod, fg, wq, wg, wm, k, v, None, None, attn_sink[l], xc, xpl, ys, conv, pool,
                      glu, wbr, wo, l, False, final)
        new_k.append(k.reshape(BP, LP, N_KV, HEAD_DIM))
        new_v.append(v.reshape(BP, LP, N_KV, HEAD_DIM))
        new_st.append(jnp.transpose(fin, (3, 1, 2, 0, 4)))

        k, v, xc, xpl, xss = _p1_call(xs, ng, mod, w1, l, True, rope_k)
        h0 = jnp.transpose(state_ssm[:, l], (3, 1, 2, 0, 4))
        (yg,) = _ssm_call(xss.reshape(BS * LS, BR_W), mats, d_row, h0, BS, LS // SSM_T)
        ys = yg.reshape(BS, LS, BR_W)
        ctx = (cache_k[:, l].reshape(BS, PAST, KV_W), cache_v[:, l].reshape(BS, PAST, KV_W))
        xs = _p3_call(xs, ng, mod, fg, wq, wg, wm, k, v, ctx, rope_q, attn_sink[l], xc, xpl, ys, conv, pool,
                      glu, wbr, wo, l, True, final)

    return (xp, xs, jnp.stack(new_k, axis=1), jnp.stack(new_v, axis=1), jnp.stack(new_st, axis=1))
```

```python
import functools
import math

import jax
import jax.numpy as jnp
from jax import lax
from jax.experimental import pallas as pl
from jax.experimental.pallas import tpu as pltpu

f32 = jnp.float32
bf16 = jnp.bfloat16
HIGHEST = lax.Precision.HIGHEST

D_MODEL = 1024
DEPTH = 2
GRID_W = 64
BR_W = 512
N_BRANCH = 4
N_HEADS = 8
N_KV = 2
HEAD_DIM = 64
Q_PER_KV = N_HEADS // N_KV
KV_W = N_KV * HEAD_DIM
WINDOW = 128
ROPE_THETA = 10000.0
NEG_INF = -1e30
CONV_K = 31
CONV_HALO = 16
POOL_WINDOWS = (2, 4, 8, 16)
POOL_GROUPS = 4
POOL_GW = BR_W // POOL_GROUPS
POOL_HALO = 8
SSM_CH = 16
SSM_GROUPS = BR_W // SSM_CH
SSM_P = 64
SSM_T = 16
SSM_TC = SSM_T * SSM_CH
EPS = 1e-6

IN_SIZES = (N_HEADS * HEAD_DIM, KV_W, KV_W, BR_W, 2 * BR_W, BR_W, BR_W, BR_W, BR_W, BR_W, N_BRANCH * D_MODEL)
IN_NAMES = ("q", "k", "v", "g_att", "a_conv", "g_conv", "x_pool", "g_pool", "x_ssm", "g_ssm", "g_mrg")
IN_OFF = {}
_o = 0
for _n, _s in zip(IN_NAMES, IN_SIZES):
    IN_OFF[_n] = (_o, _o + _s)
    _o += _s

TM = 256
MOD_ROWS = 8
V7X_VMEM_LIMIT = 56 * 1024 * 1024


def _silu(x):
    return x * jax.nn.sigmoid(x)


def _modnorm(x, g, mod):
    shift = mod[:, :D_MODEL]
    scale = mod[:, D_MODEL:2 * D_MODEL]
    xn = x * lax.rsqrt(jnp.mean(x * x, axis=-1, keepdims=True) + EPS)
    return (xn * g) * (1.0 + scale) + shift


def _dot(a, b):
    return jnp.dot(a, b, preferred_element_type=f32)


def _const_spec(shape):
    nd = len(shape)
    return pl.BlockSpec(shape, lambda *_: (0,) * nd, pipeline_mode=pl.Buffered(1))


def _mod_kernel(c_ref, w_ref, b_ref, o_ref):
    s = _silu(c_ref[...])
    o_ref[0] = jnp.dot(s, w_ref[0], precision=HIGHEST, preferred_element_type=f32) + b_ref[0]


def _mod_call(c_rows, w_ada, b_ada):
    nb = 512
    return pl.pallas_call(
        _mod_kernel,
        grid=(DEPTH, 3 * D_MODEL // nb),
        in_specs=[
            pl.BlockSpec((MOD_ROWS, D_MODEL), lambda l, j: (0, 0)),
            pl.BlockSpec((1, D_MODEL, nb), lambda l, j: (l, 0, j)),
            pl.BlockSpec((1, 1, nb), lambda l, j: (l, 0, j)),
        ],
        out_specs=pl.BlockSpec((1, MOD_ROWS, nb), lambda l, j: (l, 0, j)),
        out_shape=jax.ShapeDtypeStruct((DEPTH, MOD_ROWS, 3 * D_MODEL), f32),
        name="mod",
    )(c_rows, w_ada, b_ada.reshape(DEPTH, 1, 3 * D_MODEL))


def _rope(x, cos, sin_signed):
    width = x.shape[-1]
    lane = lax.broadcasted_iota(jnp.int32, x.shape, 1)
    first_half = (lane & (HEAD_DIM - 1)) < HEAD_DIM // 2
    from_up = pltpu.roll(x, width - HEAD_DIM // 2, 1)
    from_dn = pltpu.roll(x, HEAD_DIM // 2, 1)
    return x * cos + jnp.where(first_half, from_up, from_dn) * sin_signed


def _p1_kernel(*refs, rope):
    if rope:
        x_ref, g_ref, mod_ref, w_ref, cos_ref, sin_ref, k_ref, v_ref, xc_ref, xp_ref, xs_ref = refs
    else:
        x_ref, g_ref, mod_ref, w_ref, k_ref, v_ref, xc_ref, xp_ref, xs_ref = refs
    h = _modnorm(x_ref[...], g_ref[...], mod_ref[...]).astype(bf16)
    kv = _dot(h, w_ref[:, 0:2 * KV_W])
    k = kv[:, :KV_W]
    if rope:
        k = _rope(k, cos_ref[...], sin_ref[...])
    k_ref[...] = k
    v_ref[...] = kv[:, KV_W:]
    o = 2 * KV_W
    a = _dot(h, w_ref[:, o:o + 2 * BR_W])
    xc_ref[...] = a[:, :BR_W] * jax.nn.sigmoid(a[:, BR_W:])
    o += 2 * BR_W
    xp_ref[...] = _dot(h, w_ref[:, o:o + BR_W])
    o += BR_W
    xs_ref[...] = _dot(h, w_ref[:, o:o + BR_W])


def _p1_call(x, norm_g, mod, w1, layer, latent, rope_k):
    B, L, _ = x.shape
    nt = L // TM
    if latent:
        mod_idx = lambda b, j: (layer * MOD_ROWS + 1 + b, 0, 0)
    else:
        mod_idx = lambda b, j: (layer * MOD_ROWS, 0, 0)
    row = lambda w: pl.BlockSpec((None, TM, w), lambda b, j: (b, j, 0))
    in_specs = [
        row(D_MODEL),
        _const_spec((1, D_MODEL)),
        pl.BlockSpec((None, 1, 3 * D_MODEL), mod_idx),
        _const_spec(w1.shape),
    ]
    args = [x, norm_g, mod, w1]
    if latent:
        in_specs += [pl.BlockSpec((TM, KV_W), lambda b, j: (j, 0))] * 2
        args += list(rope_k)
    shp = lambda w: jax.ShapeDtypeStruct((B, L, w), f32)
    return pl.pallas_call(
        functools.partial(_p1_kernel, rope=latent),
        grid=(B, nt),
        in_specs=in_specs,
        out_specs=[row(KV_W), row(KV_W), row(BR_W), row(BR_W), row(BR_W)],
        out_shape=[shp(KV_W), shp(KV_W), shp(BR_W), shp(BR_W), shp(BR_W)],
        compiler_params=pltpu.CompilerParams(
            dimension_semantics=("arbitrary", "arbitrary"), vmem_limit_bytes=V7X_VMEM_LIMIT),
        name="p1_latent" if latent else "p1_context",
    )(*args)


def _ssm_disc_kernel(ldt_ref, lr_ref, li_ref, ar_ref, ai_ref):
    dt = jnp.exp(ldt_ref[...])
    mag = jnp.exp(lr_ref[...] * dt)
    ang = li_ref[...] * dt
    ar_ref[...] = mag * jnp.cos(ang)
    ai_ref[...] = mag * jnp.sin(ang)


def _ssm_disc_call(lam_re, lam_im, log_dt):
    n = 2 * SSM_GROUPS
    shp = jax.ShapeDtypeStruct((n, SSM_P), f32)
    ar, ai = pl.pallas_call(_ssm_disc_kernel, out_shape=[shp, shp], name="ssm_disc")(
        log_dt.reshape(n, 1), lam_re.reshape(n, SSM_P), lam_im.reshape(n, SSM_P))
    return ar.reshape(2, SSM_GROUPS, SSM_P), ai.reshape(2, SSM_GROUPS, SSM_P)


def _cpowers(a_re, a_im, n):
    out = [(jnp.ones_like(a_re), jnp.zeros_like(a_im))]
    for _ in range(n):
        p_re, p_im = out[-1]
        out.append((p_re * a_re - p_im * a_im, p_re * a_im + p_im * a_re))
    return out


def _ssm_prep_kernel(arc_ref, aic_ref, arr_ref, air_ref, lrr_ref, lir_ref, btr_ref, bti_ref, ctr_ref, cti_ref,
                     km_ref, bpr_ref, bpi_ref, cpr_ref, cpi_ref, ltr_ref, lti_ref):
    lane = lax.broadcasted_iota(jnp.int32, (1, SSM_TC), 1)
    lane_row = lax.broadcasted_iota(jnp.int32, (SSM_CH, SSM_TC), 1)
    blk = jnp.right_shift(lane, int(math.log2(SSM_CH)))
    for d in range(2):
        lr_r, li_r = lrr_ref[d], lir_ref[d]
        pw_r = _cpowers(arr_ref[d], air_ref[d], SSM_T)
        pw_c = _cpowers(arc_ref[d], aic_ref[d], SSM_T + 1)
        a_re, a_im = pw_r[1]
        den = lr_r * lr_r + li_r * li_r
        w_re = ((a_re - 1.0) * lr_r + a_im * li_r) / den
        w_im = (a_im * lr_r - (a_re - 1.0) * li_r) / den
        bt_re, bt_im = btr_ref[d], bti_ref[d]
        bb_re = w_re * bt_re - w_im * bt_im
        bb_im = w_re * bt_im + w_im * bt_re
        ct_re, ct_im = ctr_ref[d], cti_ref[d]
        p_re = jnp.zeros((SSM_P, SSM_TC), f32)
        p_im = jnp.zeros((SSM_P, SSM_TC), f32)
        for j in range(SSM_T):
            tau = j if d == 0 else SSM_T - 1 - j
            here = blk == j
            p_re = jnp.where(here, pw_c[tau][0], p_re)
            p_im = jnp.where(here, pw_c[tau][1], p_im)
        c_re, c_im = pw_c[1]
        q_re = p_re * c_re - p_im * c_im
        q_im = p_re * c_im + p_im * c_re
        w_re2 = p_re * ct_re - p_im * ct_im
        w_im2 = p_re * ct_im + p_im * ct_re
        grow = (jnp.dot(bb_re, w_re2, precision=HIGHEST, preferred_element_type=f32)
                - jnp.dot(bb_im, w_im2, precision=HIGHEST, preferred_element_type=f32))
        cpr_ref[d] = (q_re * ct_re - q_im * ct_im).astype(bf16)
        cpi_ref[d] = (-(q_re * ct_im + q_im * ct_re)).astype(bf16)
        for s in range(SSM_T):
            if d == 0:
                shift = SSM_CH * s
                keep = lane_row >= shift
            else:
                shift = (SSM_CH * (s + 1)) % SSM_TC
                keep = lane_row < SSM_CH * (s + 1)
            rolled = pltpu.roll(grow, shift, 1) if shift else grow
            km_ref[d, s * SSM_CH:(s + 1) * SSM_CH, :] = jnp.where(keep, rolled, 0.0).astype(bf16)
            s_re, s_im = pw_r[SSM_T - 1 - s] if d == 0 else pw_r[s]
            bpr_ref[d, s * SSM_CH:(s + 1) * SSM_CH, :] = (bb_re * s_re - bb_im * s_im).astype(bf16)
            bpi_ref[d, s * SSM_CH:(s + 1) * SSM_CH, :] = (bb_re * s_im + bb_im * s_re).astype(bf16)
        ltr_ref[d], lti_ref[d] = pw_r[SSM_T]


def _ssm_prep_call(lam_re, lam_im, log_dt, b_re, b_im, c_re, c_im):
    G, P, C = SSM_GROUPS, SSM_P, SSM_CH
    a_re, a_im = _ssm_disc_call(lam_re, lam_im, log_dt)
    col = lambda a: a.reshape(2, G, P, 1)
    rowv = lambda a: a.reshape(2, G, 1, P)
    bt = lambda a: jnp.swapaxes(a, 2, 3)
    ct = lambda a: jnp.tile(jnp.swapaxes(a, 2, 3), (1, 1, 1, SSM_T))
    spec = lambda r, c_: pl.BlockSpec((2, None, r, c_), lambda g: (0, g, 0, 0))
    return pl.pallas_call(
        _ssm_prep_kernel,
        grid=(G,),
        in_specs=[spec(P, 1), spec(P, 1), spec(1, P), spec(1, P), spec(1, P), spec(1, P),
                  spec(C, P), spec(C, P), spec(P, SSM_TC), spec(P, SSM_TC)],
        out_specs=[spec(SSM_TC, SSM_TC), spec(SSM_TC, P), spec(SSM_TC, P), spec(P, SSM_TC), spec(P, SSM_TC),
                   spec(1, P), spec(1, P)],
        out_shape=[
            jax.ShapeDtypeStruct((2, G, SSM_TC, SSM_TC), bf16),
            jax.ShapeDtypeStruct((2, G, SSM_TC, P), bf16),
            jax.ShapeDtypeStruct((2, G, SSM_TC, P), bf16),
            jax.ShapeDtypeStruct((2, G, P, SSM_TC), bf16),
            jax.ShapeDtypeStruct((2, G, P, SSM_TC), bf16),
            jax.ShapeDtypeStruct((2, G, 1, P), f32),
            jax.ShapeDtypeStruct((2, G, 1, P), f32),
        ],
        name="ssm_prep",
    )(col(a_re), col(a_im), rowv(a_re), rowv(a_im), rowv(lam_re), rowv(lam_im),
      bt(b_re), bt(b_im), ct(c_re), ct(c_im))


LANES = 128
GPT = LANES // SSM_CH


def _shift_chunks(a, k, chunk_id, nchunk, down):
    m = a.shape[0]
    if down:
        return jnp.where(chunk_id >= k, pltpu.roll(a, k, 0), 0.0)
    return jnp.where(chunk_id < nchunk - k, pltpu.roll(a, m - k, 0), 0.0)


def _ssm_kernel(*refs, batch, nchunk, has_h0):
    if has_h0:
        (x_ref, km_ref, bpr_ref, bpi_ref, cpr_ref, cpi_ref, ltr_ref, lti_ref, d_ref, h0_ref, y_ref,
         abuf, ybuf) = refs
    else:
        (x_ref, km_ref, bpr_ref, bpi_ref, cpr_ref, cpi_ref, ltr_ref, lti_ref, d_ref, y_ref, fin_ref,
         abuf, ybuf, sbuf) = refs
    gg = pl.program_id(1)
    m = batch * nchunk
    slot = jnp.right_shift(lax.broadcasted_iota(jnp.int32, (1, LANES), 1), int(math.log2(SSM_CH)))

    @pl.when(gg == 0)
    def _():
        for s in range(SSM_T):
            a_s = x_ref[pl.ds(s, m, stride=SSM_T), :]
            ss = s % GPT
            abuf[s] = pltpu.roll(a_s, ss * SSM_CH, 1) if ss else a_s

    tiles = []
    for hh in range(SSM_T // GPT):
        acc = jnp.zeros((m, LANES), f32)
        for ss in range(GPT):
            acc = jnp.where(slot == ((gg + ss) & (GPT - 1)), abuf[hh * GPT + ss], acc)
        tiles.append(pltpu.roll(acc, (LANES - gg * SSM_CH) & (LANES - 1), 1))
    xb = jnp.concatenate(tiles, axis=1).astype(bf16)

    row = lax.broadcasted_iota(jnp.int32, (m, SSM_P), 0)
    chunk_id = row & (nchunk - 1)
    y = None
    for d in range(2):
        down = d == 0
        yk = _dot(xb, km_ref[d])
        y = yk if y is None else y + yk
        e_re = _dot(xb, bpr_ref[d])
        e_im = _dot(xb, bpi_ref[d])
        a_re, a_im = ltr_ref[d], lti_ref[d]
        entry = chunk_id == (0 if down else nchunk - 1)
        if has_h0:
            h_re_t = jnp.zeros((m, SSM_P), f32)
            h_im_t = jnp.zeros((m, SSM_P), f32)
            for b in range(batch):
                mine = jnp.right_shift(row, int(math.log2(nchunk))) == b
                h_re_t = jnp.where(mine, h0_ref[d, 0, b:b + 1, :], h_re_t)
                h_im_t = jnp.where(mine, h0_ref[d, 1, b:b + 1, :], h_im_t)
            e_re = e_re + jnp.where(entry, a_re * h_re_t - a_im * h_im_t, 0.0)
            e_im = e_im + jnp.where(entry, a_re * h_im_t + a_im * h_re_t, 0.0)
        s_re, s_im = e_re, e_im
        step = 1
        while step < nchunk:
            p_re = _shift_chunks(s_re, step, chunk_id, nchunk, down)
            p_im = _shift_chunks(s_im, step, chunk_id, nchunk, down)
            s_re, s_im = (s_re + a_re * p_re - a_im * p_im, s_im + a_re * p_im + a_im * p_re)
            a_re, a_im = a_re * a_re - a_im * a_im, 2.0 * a_re * a_im
            step *= 2
        hs_re = _shift_chunks(s_re, 1, chunk_id, nchunk, down)
        hs_im = _shift_chunks(s_im, 1, chunk_id, nchunk, down)
        if has_h0:
            hs_re = jnp.where(entry, h_re_t, hs_re)
            hs_im = jnp.where(entry, h_im_t, hs_im)
        else:
            sbuf[0] = s_re
            sbuf[1] = s_im
            last = nchunk - 1 if down else 0
            fin_ref[d, 0] = sbuf[0, pl.ds(last, batch, stride=nchunk), :]
            fin_ref[d, 1] = sbuf[1, pl.ds(last, batch, stride=nchunk), :]
        y = y + _dot(hs_re.astype(bf16), cpr_ref[d]) + _dot(hs_im.astype(bf16), cpi_ref[d])

    up = gg * SSM_CH
    ybuf[gg] = jnp.concatenate([pltpu.roll(y[:, :LANES], up, 1), pltpu.roll(y[:, LANES:], up, 1)], axis=1)

    @pl.when(gg == GPT - 1)
    def _():
        for hh in range(SSM_T // GPT):
            for tt in range(GPT):
                t = hh * GPT + tt
                acc = jnp.zeros((m, LANES), f32)
                for g2 in range(GPT):
                    acc = jnp.where(slot == ((g2 + tt) & (GPT - 1)), ybuf[g2, :, hh * LANES:(hh + 1) * LANES], acc)
                d_t = pltpu.roll(d_ref[...], tt * SSM_CH, 1) if tt else d_ref[...]
                tot = acc + d_t * abuf[t]
                y_ref[pl.ds(t, m, stride=SSM_T), :] = pltpu.roll(tot, LANES - tt * SSM_CH, 1) if tt else tot


def _ssm_call(x2d, mats, d_row, h0, batch, nchunk):
    G, P = SSM_GROUPS, SSM_P
    m = batch * nchunk
    rows = m * SSM_T
    km, bpr, bpi, cpr, cpi, ltr, lti = mats
    spec = lambda r, c_: pl.BlockSpec((2, None, r, c_), lambda q, gg: (0, q * GPT + gg, 0, 0))
    tile = pl.BlockSpec((rows, LANES), lambda q, gg: (0, q))
    in_specs = [
        tile,
        spec(SSM_TC, SSM_TC), spec(SSM_TC, P), spec(SSM_TC, P), spec(P, SSM_TC), spec(P, SSM_TC),
        spec(1, P), spec(1, P),
        pl.BlockSpec((1, LANES), lambda q, gg: (0, q)),
    ]
    args = [x2d, km, bpr, bpi, cpr, cpi, ltr, lti, d_row]
    out_specs = [tile]
    out_shape = [jax.ShapeDtypeStruct((rows, G * SSM_CH), f32)]
    scratch = [pltpu.VMEM((SSM_T, m, LANES), f32), pltpu.VMEM((GPT, m, SSM_TC), f32)]
    st_spec = pl.BlockSpec((None, 2, 2, batch, P), lambda q, gg: (q * GPT + gg, 0, 0, 0, 0))
    if h0 is not None:
        in_specs.append(st_spec)
        args.append(h0)
    else:
        out_specs.append(st_spec)
        out_shape.append(jax.ShapeDtypeStruct((G, 2, 2, batch, P), f32))
        scratch.append(pltpu.VMEM((2, m, P), f32))
    return pl.pallas_call(
        functools.partial(_ssm_kernel, batch=batch, nchunk=nchunk, has_h0=h0 is not None),
        grid=(G // GPT, GPT),
        in_specs=in_specs,
        out_specs=out_specs,
        out_shape=out_shape,
        scratch_shapes=scratch,
        compiler_params=pltpu.CompilerParams(dimension_semantics=("arbitrary", "arbitrary")),
        name="ssm_latent" if h0 is not None else "ssm_context",
    )(*args)


def _attention(q, keys, sink_ref, masks):
    rows = q.shape[0]
    lane = lax.broadcasted_iota(jnp.int32, (1, KV_W), 1)
    half = [lane < HEAD_DIM, lane >= HEAD_DIM]
    kb = []
    for k, v in keys:
        k_sw = pltpu.roll(k, HEAD_DIM, 1)
        v_sw = pltpu.roll(v, HEAD_DIM, 1)
        kb.append(((k.astype(bf16), k_sw.astype(bf16)), (v, v_sw)))
    outs = []
    for pair in range(N_HEADS // 2):
        q_pair = q[:, pair * KV_W:(pair + 1) * KV_W]
        o_pair = None
        for pp in range(2):
            hd = 2 * pair + pp
            kh = hd // Q_PER_KV
            src = 0 if kh == pp else 1
            qh = jnp.where(half[pp], q_pair, 0.0).astype(bf16)
            sink = sink_ref[hd]
            scores = []
            m = jnp.zeros((rows, 1), f32) + sink
            for (kk, _), mask in zip(kb, masks):
                s = lax.dot_general(qh, kk[src], (((1,), (1,)), ((), ())), preferred_element_type=f32)
                if mask is not None:
                    s = s + mask
                scores.append(s)
                m = jnp.maximum(m, jnp.max(s, axis=-1, keepdims=True))
            den = jnp.exp(sink - m)
            acc = None
            for s, (_, vv) in zip(scores, kb):
                e = jnp.exp(s - m)
                den = den + jnp.sum(e, axis=-1, keepdims=True)
                o = _dot(e.astype(bf16), jnp.where(half[pp], vv[src], 0.0).astype(bf16))
                acc = o if acc is None else acc + o
            acc = acc / den
            o_pair = acc if o_pair is None else o_pair + acc
        outs.append(o_pair)
    return jnp.concatenate(outs, axis=-1)


def _p3_kernel(*refs, latent, final, seq_len):
    it = iter(refs)
    x_ref, g_ref, mod_ref, fg_ref, wq_ref, wg_ref, wm_ref = (next(it) for _ in range(7))
    if latent:
        kp_ref, kc_ref, kn_ref, vp_ref, vc_ref, vn_ref, ck_ref, cv_ref, cos_ref, sin_ref = (next(it) for _ in range(10))
    else:
        kc_ref, vc_ref = next(it), next(it)
    sink_ref = next(it)
    xcp_ref, xcc_ref, xcn_ref, xpp_ref, xpc_ref, xpn_ref, ys_ref = (next(it) for _ in range(7))
    dw_ref, db_ref, lg_ref, lb_ref, pw_ref, ps_ref, glu_ref, wbr_ref, wo_ref = (next(it) for _ in range(9))
    o_ref = next(it)
    cbuf, pbuf = next(it), next(it)

    j = pl.program_id(1)
    nt = pl.num_programs(1)
    has_prev = j > 0
    has_next = j < nt - 1
    t0 = j * TM

    x = x_ref[...]
    mod = mod_ref[...]
    hb = _modnorm(x, g_ref[...], mod).astype(bf16)

    q = _dot(hb, wq_ref[...]) * HEAD_DIM ** -0.5
    if latent:
        q = _rope(q, cos_ref[...], sin_ref[...])
        k_loc = jnp.concatenate([kp_ref[...], kc_ref[...], kn_ref[...]], axis=0)
        v_loc = jnp.concatenate([vp_ref[...], vc_ref[...], vn_ref[...]], axis=0)
        span = TM + 2 * WINDOW
        qpos = t0 + lax.broadcasted_iota(jnp.int32, (TM, span), 0)
        kpos = t0 - WINDOW + lax.broadcasted_iota(jnp.int32, (TM, span), 1)
        band = (jnp.abs(qpos - kpos) <= WINDOW) & (kpos >= 0) & (kpos < seq_len)
        bias = jnp.where(band, 0.0, NEG_INF)
        o_att = _attention(q, [(ck_ref[...], cv_ref[...]), (k_loc, v_loc)], sink_ref, [None, bias])
    else:
        o_att = _attention(q, [(kc_ref[...], vc_ref[...])], sink_ref, [None])

    gates = _dot(hb, wg_ref[...])

    cbuf[0:CONV_HALO, :] = jnp.where(has_prev, xcp_ref[...], 0.0)
    cbuf[CONV_HALO:CONV_HALO + TM, :] = xcc_ref[...]
    cbuf[CONV_HALO + TM:, :] = jnp.where(has_next, xcn_ref[...], 0.0)
    sub = 8
    base = CONV_HALO - CONV_K // 2
    acc = None
    for r in range(sub):
        part = None
        for a in range((base + CONV_K - 1) // sub + 1):
            tap = sub * a + r - base
            if 0 <= tap < CONV_K:
                term = cbuf[sub * a:sub * a + TM + sub, :] * dw_ref[tap:tap + 1, :]
                part = term if part is None else part + term
        if part is not None:
            acc = part[r:r + TM, :] if acc is None else acc + part[r:r + TM, :]
    acc = acc + db_ref[...]
    mu = jnp.mean(acc, axis=-1, keepdims=True)
    cen = acc - mu
    yn = cen * lax.rsqrt(jnp.mean(cen * cen, axis=-1, keepdims=True) + EPS)
    o_conv = _silu(yn * lg_ref[...] + lb_ref[...])

    pbuf[0:POOL_HALO, :] = jnp.where(has_prev, xpp_ref[...], 0.0)
    pbuf[POOL_HALO:POOL_HALO + TM, :] = xpc_ref[...]
    pbuf[POOL_HALO + TM:, :] = jnp.where(has_next, xpn_ref[...], 0.0)
    tpos = t0 + lax.broadcasted_iota(jnp.int32, (TM, 1), 0)
    pooled = []
    for gi, w in enumerate(POOL_WINDOWS):
        lanes = slice(gi * POOL_GW, (gi + 1) * POOL_GW)
        tot = None
        for r in range(-(w // 2), w - w // 2):
            piece = pbuf[POOL_HALO + r:POOL_HALO + r + TM, lanes]
            tot = piece if tot is None else tot + piece
        lo = jnp.maximum(tpos - w // 2, 0)
        hi = jnp.minimum(tpos + (w - w // 2), seq_len)
        cnt = (hi - lo).astype(f32)
        dlt = tot / cnt - pbuf[POOL_HALO:POOL_HALO + TM, lanes]
        pooled.append(_dot(dlt.astype(bf16), pw_ref[gi]))
    o_pool = jnp.concatenate(pooled, axis=-1) * ps_ref[...]

    z = _dot(jax.nn.gelu(ys_ref[...]).astype(bf16), glu_ref[...])
    o_ssm = z[:, :BR_W] * jax.nn.sigmoid(z[:, BR_W:])

    merged = None
    for n, o_n in enumerate((o_att, o_conv, o_pool, o_ssm)):
        br = (o_n * _silu(gates[:, n * BR_W:(n + 1) * BR_W])).astype(bf16)
        proj = _dot(br, wbr_ref[n])
        mg = jax.nn.sigmoid(_dot(hb, wm_ref[:, n * D_MODEL:(n + 1) * D_MODEL]))
        merged = mg * proj if merged is None else merged + mg * proj
    out = x + mod[:, 2 * D_MODEL:] * _dot(merged.astype(bf16), wo_ref[...])
    if final:
        out = out * lax.rsqrt(jnp.mean(out * out, axis=-1, keepdims=True) + EPS) * fg_ref[...]
    o_ref[...] = out


def _p3_call(x, norm_g, mod, final_g, wq, wg, wm, k, v, ctx_kv, rope_q, sink, xc, xp, ys, conv, pool, glu_w,
             w_br, w_out, layer, latent, final):
    B, L, _ = x.shape
    nt = L // TM
    if latent:
        mod_idx = lambda b, j: (layer * MOD_ROWS + 1 + b, 0, 0)
    else:
        mod_idx = lambda b, j: (layer * MOD_ROWS, 0, 0)
    row = lambda w: pl.BlockSpec((None, TM, w), lambda b, j: (b, j, 0))

    def halo(rows, w, side):
        per = TM // rows
        last = L // rows - 1
        if side < 0:
            return pl.BlockSpec((None, rows, w), lambda b, j: (b, jnp.maximum(j * per - 1, 0), 0))
        return pl.BlockSpec((None, rows, w), lambda b, j: (b, jnp.minimum((j + 1) * per, last), 0))

    in_specs = [row(D_MODEL), _const_spec((1, D_MODEL)), pl.BlockSpec((None, 1, 3 * D_MODEL), mod_idx),
                _const_spec((1, D_MODEL)), _const_spec(wq.shape), _const_spec(wg.shape), _const_spec(wm.shape)]
    args = [x, norm_g, mod, final_g, wq, wg, wm]
    if latent:
        ck, cv = ctx_kv
        kv_c = pl.BlockSpec((None, ck.shape[1], KV_W), lambda b, j: (b, 0, 0))
        in_specs += [halo(WINDOW, KV_W, -1), row(KV_W), halo(WINDOW, KV_W, 1),
                     halo(WINDOW, KV_W, -1), row(KV_W), halo(WINDOW, KV_W, 1), kv_c, kv_c,
                     pl.BlockSpec((TM, N_HEADS * HEAD_DIM), lambda b, j: (j, 0)),
                     pl.BlockSpec((TM, N_HEADS * HEAD_DIM), lambda b, j: (j, 0))]
        args += [k, k, k, v, v, v, ck, cv, rope_q[0], rope_q[1]]
    else:
        in_specs += [row(KV_W), row(KV_W)]
        args += [k, v]
    in_specs.append(pl.BlockSpec(memory_space=pltpu.SMEM))
    args.append(sink)
    in_specs += [halo(CONV_HALO, BR_W, -1), row(BR_W), halo(CONV_HALO, BR_W, 1),
                 halo(POOL_HALO, BR_W, -1), row(BR_W), halo(POOL_HALO, BR_W, 1), row(BR_W)]
    args += [xc, xc, xc, xp, xp, xp, ys]
    tail = list(conv) + list(pool) + [glu_w, w_br, w_out]
    in_specs += [_const_spec(a.shape) for a in tail]
    args += tail
    return pl.pallas_call(
        functools.partial(_p3_kernel, latent=latent, final=final, seq_len=L),
        grid=(B, nt),
        in_specs=in_specs,
        out_specs=row(D_MODEL),
        out_shape=jax.ShapeDtypeStruct((B, L, D_MODEL), f32),
        scratch_shapes=[pltpu.VMEM((TM + 2 * CONV_HALO, BR_W), f32), pltpu.VMEM((TM + 2 * POOL_HALO, BR_W), f32)],
        compiler_params=pltpu.CompilerParams(
            dimension_semantics=("arbitrary", "arbitrary"), vmem_limit_bytes=V7X_VMEM_LIMIT),
        name="p3_latent" if latent else "p3_context",
    )(*args)


def _rope_tables(L, heads):
    rows = L // GRID_W
    row = jnp.repeat(jnp.arange(rows, dtype=f32), GRID_W)
    col = jnp.tile(jnp.arange(GRID_W, dtype=f32), rows)
    n_freq = HEAD_DIM // 4
    inv = ROPE_THETA ** (-jnp.arange(n_freq, dtype=f32) / n_freq)
    ang = jnp.concatenate([row[:, None] * inv, col[:, None] * inv], axis=-1)
    cos, sin = jnp.cos(ang), jnp.sin(ang)
    return (jnp.tile(jnp.concatenate([cos, cos], axis=-1), (1, heads)),
            jnp.tile(jnp.concatenate([-sin, sin], axis=-1), (1, heads)))


def _cols(w, *names):
    return jnp.concatenate([w[:, IN_OFF[n][0]:IN_OFF[n][1]] for n in names], axis=1).astype(bf16)


def kernel(x_prompt, x_sample, cache_k, cache_v, state_ssm, c, c_ctx, norm_g, w_ada, b_ada, w_in, attn_sink, conv_dw, conv_db, conv_ln_g, conv_ln_b, pool_w, pool_scale, ssm_lam_re, ssm_lam_im, ssm_log_dt, ssm_b_re, ssm_b_im, ssm_c_re, ssm_c_im, ssm_d, ssm_glu_w, w_br, w_out, final_g):
    BP, LP, _ = x_prompt.shape
    BS, LS, _ = x_sample.shape
    PAST = cache_k.shape[2]
    c_rows = jnp.concatenate([c_ctx[None, :], c, jnp.zeros((MOD_ROWS - 1 - BS, D_MODEL), f32)], axis=0)
    mod = _mod_call(c_rows, w_ada, b_ada).reshape(DEPTH * MOD_ROWS, 1, 3 * D_MODEL)
    rope_k = _rope_tables(LS, N_KV)
    rope_q = _rope_tables(LS, N_HEADS)
    fg = final_g.reshape(1, D_MODEL)

    xp, xs = x_prompt, x_sample
    new_k, new_v, new_st = [], [], []
    for l in range(DEPTH):
        ng = norm_g[l].reshape(1, D_MODEL)
        w1 = _cols(w_in[l], "k", "v", "a_conv", "x_pool", "x_ssm")
        wq = _cols(w_in[l], "q")
        wg = _cols(w_in[l], "g_att", "g_conv", "g_pool", "g_ssm")
        wm = _cols(w_in[l], "g_mrg")
        mats = _ssm_prep_call(ssm_lam_re[l], ssm_lam_im[l], ssm_log_dt[l], ssm_b_re[l], ssm_b_im[l],
                              ssm_c_re[l], ssm_c_im[l])
        d_row = ssm_d[l].reshape(1, BR_W)
        dw = jnp.concatenate([conv_dw[l], jnp.zeros((1, BR_W), f32)], axis=0)
        conv = (dw, conv_db[l].reshape(1, BR_W), conv_ln_g[l].reshape(1, BR_W), conv_ln_b[l].reshape(1, BR_W))
        pool = (pool_w[l].astype(bf16), pool_scale[l].reshape(1, BR_W))
        glu = ssm_glu_w[l].astype(bf16)
        wbr = w_br[l].astype(bf16)
        wo = w_out[l].astype(bf16)
        final = l == DEPTH - 1

        k, v, xc, xpl, xss = _p1_call(xp, ng, mod, w1, l, False, None)
        yg, fin = _ssm_call(xss.reshape(BP * LP, BR_W), mats, d_row, None, BP, LP // SSM_T)
        ys = yg.reshape(BP, LP, BR_W)
        xp = _p3_call(xp, ng, mod, fg, wq, wg, wm, k, v, None, None, attn_sink[l], xc, xpl, ys, conv, pool,
                      glu, wbr, wo, l, False, final)
        new_k.append(k.reshape(BP, LP, N_KV, HEAD_DIM))
        new_v.append(v.reshape(BP, LP, N_KV, HEAD_DIM))
        new_st.append(jnp.transpose(fin, (3, 1, 2, 0, 4)))

        k, v, xc, xpl, xss = _p1_call(xs, ng, mod, w1, l, True, rope_k)
        h0 = jnp.transpose(state_ssm[:, l], (3, 1, 2, 0, 4))
        (yg,) = _ssm_call(xss.reshape(BS * LS, BR_W), mats, d_row, h0, BS, LS // SSM_T)
        ys = yg.reshape(BS, LS, BR_W)
        ctx = (cache_k[:, l].reshape(BS, PAST, KV_W), cache_v[:, l].reshape(BS, PAST, KV_W))
        xs = _p3_call(xs, ng, mod, fg, wq, wg, wm, k, v, ctx, rope_q, attn_sink[l], xc, xpl, ys, conv, pool,
                      glu, wbr, wo, l, True, final)

    return (xp, xs, jnp.stack(new_k, axis=1), jnp.stack(new_v, axis=1), jnp.stack(new_st, axis=1))
```

```python
import functools
import math

import jax
import jax.numpy as jnp
from jax import lax
from jax.experimental import pallas as pl
from jax.experimental.pallas import tpu as pltpu

f32 = jnp.float32
bf16 = jnp.bfloat16
HIGHEST = lax.Precision.HIGHEST

D_MODEL = 1024
DEPTH = 2
GRID_W = 64
BR_W = 512
N_BRANCH = 4
N_HEADS = 8
N_KV = 2
HEAD_DIM = 64
Q_PER_KV = N_HEADS // N_KV
KV_W = N_KV * HEAD_DIM
WINDOW = 128
ROPE_THETA = 10000.0
NEG_INF = -1e30
CONV_K = 31
POOL_WINDOWS = (2, 4, 8, 16)
POOL_GROUPS = 4
POOL_GW = BR_W // POOL_GROUPS
HALO = 16
SSM_CH = 16
SSM_GROUPS = BR_W // SSM_CH
SSM_P = 64
SSM_T = 16
SSM_TC = SSM_T * SSM_CH
EPS = 1e-6

IN_SIZES = (N_HEADS * HEAD_DIM, KV_W, KV_W, BR_W, 2 * BR_W, BR_W, BR_W, BR_W, BR_W, BR_W, N_BRANCH * D_MODEL)
IN_NAMES = ("q", "k", "v", "g_att", "a_conv", "g_conv", "x_pool", "g_pool", "x_ssm", "g_ssm", "g_mrg")
IN_OFF = {}
_o = 0
for _n, _s in zip(IN_NAMES, IN_SIZES):
    IN_OFF[_n] = (_o, _o + _s)
    _o += _s

TM = 256
MOD_ROWS = 8
V7X_VMEM_LIMIT = 56 * 1024 * 1024


def _silu(x):
    return x * jax.nn.sigmoid(x)


def _modnorm(x, g, mod):
    shift = mod[:, :D_MODEL]
    scale = mod[:, D_MODEL:2 * D_MODEL]
    xn = x * lax.rsqrt(jnp.mean(x * x, axis=-1, keepdims=True) + EPS)
    return (xn * g) * (1.0 + scale) + shift


def _dot(a, b):
    return jnp.dot(a, b, preferred_element_type=f32)


def _const_spec(shape):
    nd = len(shape)
    return pl.BlockSpec(shape, lambda *_: (0,) * nd, pipeline_mode=pl.Buffered(1))


def _mod_kernel(c_ref, w_ref, b_ref, o_ref):
    s = _silu(c_ref[...])
    o_ref[0] = jnp.dot(s, w_ref[0], precision=HIGHEST, preferred_element_type=f32) + b_ref[0]


def _mod_call(c_rows, w_ada, b_ada):
    nb = 512
    return pl.pallas_call(
        _mod_kernel,
        grid=(DEPTH, 3 * D_MODEL // nb),
        in_specs=[
            pl.BlockSpec((MOD_ROWS, D_MODEL), lambda l, j: (0, 0)),
            pl.BlockSpec((1, D_MODEL, nb), lambda l, j: (l, 0, j)),
            pl.BlockSpec((1, 1, nb), lambda l, j: (l, 0, j)),
        ],
        out_specs=pl.BlockSpec((1, MOD_ROWS, nb), lambda l, j: (l, 0, j)),
        out_shape=jax.ShapeDtypeStruct((DEPTH, MOD_ROWS, 3 * D_MODEL), f32),
        name="mod",
    )(c_rows, w_ada, b_ada.reshape(DEPTH, 1, 3 * D_MODEL))


def _rope(x, cos, sin_signed):
    width = x.shape[-1]
    lane = lax.broadcasted_iota(jnp.int32, x.shape, 1)
    first_half = (lane & (HEAD_DIM - 1)) < HEAD_DIM // 2
    from_up = pltpu.roll(x, width - HEAD_DIM // 2, 1)
    from_dn = pltpu.roll(x, HEAD_DIM // 2, 1)
    return x * cos + jnp.where(first_half, from_up, from_dn) * sin_signed


def _p1_kernel(*refs, rope, seq_len):
    it = iter(refs)
    x_ref, xprev_ref, xnext_ref, g_ref, mod_ref, w_ref = (next(it) for _ in range(6))
    if rope:
        cos_ref, sin_ref = next(it), next(it)
    dw_ref, db_ref, lg_ref, lb_ref, pw_ref, ps_ref = (next(it) for _ in range(6))
    k_ref, v_ref, oc_ref, op_ref, xs_ref = (next(it) for _ in range(5))
    cbuf, pbuf = next(it), next(it)

    j = pl.program_id(1)
    has_prev = j > 0
    has_next = j < pl.num_programs(1) - 1
    g, mod = g_ref[...], mod_ref[...]
    h = _modnorm(x_ref[...], g, mod).astype(bf16)
    x_halo = jnp.concatenate([xprev_ref[...], xnext_ref[...]], axis=0)
    h_halo = _modnorm(x_halo, g, mod).astype(bf16)
    row_h = lax.broadcasted_iota(jnp.int32, (2 * HALO, 1), 0)
    live = jnp.where(row_h < HALO, has_prev.astype(jnp.int32), has_next.astype(jnp.int32)) > 0

    kv = _dot(h, w_ref[:, 0:2 * KV_W])
    k = kv[:, :KV_W]
    if rope:
        k = _rope(k, cos_ref[...], sin_ref[...])
    k_ref[...] = k
    v_ref[...] = kv[:, KV_W:]

    o = 2 * KV_W
    glu = lambda a: a[:, :BR_W] * jax.nn.sigmoid(a[:, BR_W:])
    xc_halo = jnp.where(live, glu(_dot(h_halo, w_ref[:, o:o + 2 * BR_W])), 0.0)
    cbuf[0:HALO, :] = xc_halo[:HALO]
    cbuf[HALO:HALO + TM, :] = glu(_dot(h, w_ref[:, o:o + 2 * BR_W]))
    cbuf[HALO + TM:, :] = xc_halo[HALO:]
    sub = 8
    base = HALO - CONV_K // 2
    acc = None
    for r in range(sub):
        part = None
        for a in range((base + CONV_K - 1) // sub + 1):
            tap = sub * a + r - base
            if 0 <= tap < CONV_K:
                term = cbuf[sub * a:sub * a + TM + sub, :] * dw_ref[tap:tap + 1, :]
                part = term if part is None else part + term
        if part is not None:
            acc = part[r:r + TM, :] if acc is None else acc + part[r:r + TM, :]
    acc = acc + db_ref[...]
    mu = jnp.mean(acc, axis=-1, keepdims=True)
    cen = acc - mu
    yn = cen * lax.rsqrt(jnp.mean(cen * cen, axis=-1, keepdims=True) + EPS)
    oc_ref[...] = _silu(yn * lg_ref[...] + lb_ref[...])

    o += 2 * BR_W
    xp_halo = jnp.where(live, _dot(h_halo, w_ref[:, o:o + BR_W]), 0.0)
    pbuf[0:HALO, :] = xp_halo[:HALO]
    pbuf[HALO:HALO + TM, :] = _dot(h, w_ref[:, o:o + BR_W])
    pbuf[HALO + TM:, :] = xp_halo[HALO:]
    tpos = j * TM + lax.broadcasted_iota(jnp.int32, (TM, 1), 0)
    for gi, w in enumerate(POOL_WINDOWS):
        lanes = slice(gi * POOL_GW, (gi + 1) * POOL_GW)
        tot = None
        for r in range(-(w // 2), w - w // 2):
            piece = pbuf[HALO + r:HALO + r + TM, lanes]
            tot = piece if tot is None else tot + piece
        lo = jnp.maximum(tpos - w // 2, 0)
        hi = jnp.minimum(tpos + (w - w // 2), seq_len)
        cnt = (hi - lo).astype(f32)
        dlt = tot / cnt - pbuf[HALO:HALO + TM, lanes]
        op_ref[:, lanes] = _dot(dlt.astype(bf16), pw_ref[gi]) * ps_ref[:, lanes]

    o += BR_W
    xs_ref[...] = _dot(h, w_ref[:, o:o + BR_W])


def _p1_call(x, norm_g, mod, w1, conv, pool, layer, latent, rope_k):
    B, L, _ = x.shape
    nt = L // TM
    if latent:
        mod_idx = lambda b, j: (layer * MOD_ROWS + 1 + b, 0, 0)
    else:
        mod_idx = lambda b, j: (layer * MOD_ROWS, 0, 0)
    row = lambda w: pl.BlockSpec((None, TM, w), lambda b, j: (b, j, 0))
    per = TM // HALO
    last = L // HALO - 1
    in_specs = [
        row(D_MODEL),
        pl.BlockSpec((None, HALO, D_MODEL), lambda b, j: (b, jnp.maximum(j * per - 1, 0), 0)),
        pl.BlockSpec((None, HALO, D_MODEL), lambda b, j: (b, jnp.minimum((j + 1) * per, last), 0)),
        _const_spec((1, D_MODEL)),
        pl.BlockSpec((None, 1, 3 * D_MODEL), mod_idx),
        _const_spec(w1.shape),
    ]
    args = [x, x, x, norm_g, mod, w1]
    if latent:
        in_specs += [pl.BlockSpec((TM, KV_W), lambda b, j: (j, 0))] * 2
        args += list(rope_k)
    tail = list(conv) + list(pool)
    in_specs += [_const_spec(a.shape) for a in tail]
    args += tail
    shp = lambda w: jax.ShapeDtypeStruct((B, L, w), f32)
    return pl.pallas_call(
        functools.partial(_p1_kernel, rope=latent, seq_len=L),
        grid=(B, nt),
        in_specs=in_specs,
        out_specs=[row(KV_W), row(KV_W), row(BR_W), row(BR_W), row(BR_W)],
        out_shape=[shp(KV_W), shp(KV_W), shp(BR_W), shp(BR_W), shp(BR_W)],
        scratch_shapes=[pltpu.VMEM((TM + 2 * HALO, BR_W), f32), pltpu.VMEM((TM + 2 * HALO, BR_W), f32)],
        compiler_params=pltpu.CompilerParams(
            dimension_semantics=("arbitrary", "arbitrary"), vmem_limit_bytes=V7X_VMEM_LIMIT),
        name="p1_latent" if latent else "p1_context",
    )(*args)


def _ssm_disc_kernel(ldt_ref, lr_ref, li_ref, ar_ref, ai_ref):
    dt = jnp.exp(ldt_ref[...])
    mag = jnp.exp(lr_ref[...] * dt)
    ang = li_ref[...] * dt
    ar_ref[...] = mag * jnp.cos(ang)
    ai_ref[...] = mag * jnp.sin(ang)


def _ssm_disc_call(lam_re, lam_im, log_dt):
    n = 2 * SSM_GROUPS
    shp = jax.ShapeDtypeStruct((n, SSM_P), f32)
    ar, ai = pl.pallas_call(_ssm_disc_kernel, out_shape=[shp, shp], name="ssm_disc")(
        log_dt.reshape(n, 1), lam_re.reshape(n, SSM_P), lam_im.reshape(n, SSM_P))
    return ar.reshape(2, SSM_GROUPS, SSM_P), ai.reshape(2, SSM_GROUPS, SSM_P)


def _cpowers(a_re, a_im, n):
    out = [(jnp.ones_like(a_re), jnp.zeros_like(a_im))]
    for _ in range(n):
        p_re, p_im = out[-1]
        out.append((p_re * a_re - p_im * a_im, p_re * a_im + p_im * a_re))
    return out


def _ssm_prep_kernel(arc_ref, aic_ref, arr_ref, air_ref, lrr_ref, lir_ref, btr_ref, bti_ref, ctr_ref, cti_ref,
                     km_ref, bpr_ref, bpi_ref, cpr_ref, cpi_ref, ltr_ref, lti_ref):
    lane = lax.broadcasted_iota(jnp.int32, (1, SSM_TC), 1)
    lane_row = lax.broadcasted_iota(jnp.int32, (SSM_CH, SSM_TC), 1)
    blk = jnp.right_shift(lane, int(math.log2(SSM_CH)))
    for d in range(2):
        lr_r, li_r = lrr_ref[d], lir_ref[d]
        pw_r = _cpowers(arr_ref[d], air_ref[d], SSM_T)
        pw_c = _cpowers(arc_ref[d], aic_ref[d], SSM_T + 1)
        a_re, a_im = pw_r[1]
        den = lr_r * lr_r + li_r * li_r
        w_re = ((a_re - 1.0) * lr_r + a_im * li_r) / den
        w_im = (a_im * lr_r - (a_re - 1.0) * li_r) / den
        bt_re, bt_im = btr_ref[d], bti_ref[d]
        bb_re = w_re * bt_re - w_im * bt_im
        bb_im = w_re * bt_im + w_im * bt_re
        ct_re, ct_im = ctr_ref[d], cti_ref[d]
        p_re = jnp.zeros((SSM_P, SSM_TC), f32)
        p_im = jnp.zeros((SSM_P, SSM_TC), f32)
        for j in range(SSM_T):
            tau = j if d == 0 else SSM_T - 1 - j
            here = blk == j
            p_re = jnp.where(here, pw_c[tau][0], p_re)
            p_im = jnp.where(here, pw_c[tau][1], p_im)
        c_re, c_im = pw_c[1]
        q_re = p_re * c_re - p_im * c_im
        q_im = p_re * c_im + p_im * c_re
        w_re2 = p_re * ct_re - p_im * ct_im
        w_im2 = p_re * ct_im + p_im * ct_re
        grow = (jnp.dot(bb_re, w_re2, precision=HIGHEST, preferred_element_type=f32)
                - jnp.dot(bb_im, w_im2, precision=HIGHEST, preferred_element_type=f32))
        cpr_ref[d] = (q_re * ct_re - q_im * ct_im).astype(bf16)
        cpi_ref[d] = (-(q_re * ct_im + q_im * ct_re)).astype(bf16)
        for s in range(SSM_T):
            if d == 0:
                shift = SSM_CH * s
                keep = lane_row >= shift
            else:
                shift = (SSM_CH * (s + 1)) % SSM_TC
                keep = lane_row < SSM_CH * (s + 1)
            rolled = pltpu.roll(grow, shift, 1) if shift else grow
            km_ref[d, s * SSM_CH:(s + 1) * SSM_CH, :] = jnp.where(keep, rolled, 0.0).astype(bf16)
            s_re, s_im = pw_r[SSM_T - 1 - s] if d == 0 else pw_r[s]
            bpr_ref[d, s * SSM_CH:(s + 1) * SSM_CH, :] = (bb_re * s_re - bb_im * s_im).astype(bf16)
            bpi_ref[d, s * SSM_CH:(s + 1) * SSM_CH, :] = (bb_re * s_im + bb_im * s_re).astype(bf16)
        ltr_ref[d], lti_ref[d] = pw_r[SSM_T]


def _ssm_prep_call(lam_re, lam_im, log_dt, b_re, b_im, c_re, c_im):
    G, P, C = SSM_GROUPS, SSM_P, SSM_CH
    a_re, a_im = _ssm_disc_call(lam_re, lam_im, log_dt)
    col = lambda a: a.reshape(2, G, P, 1)
    rowv = lambda a: a.reshape(2, G, 1, P)
    bt = lambda a: jnp.swapaxes(a, 2, 3)
    ct = lambda a: jnp.tile(jnp.swapaxes(a, 2, 3), (1, 1, 1, SSM_T))
    spec = lambda r, c_: pl.BlockSpec((2, None, r, c_), lambda g: (0, g, 0, 0))
    return pl.pallas_call(
        _ssm_prep_kernel,
        grid=(G,),
        in_specs=[spec(P, 1), spec(P, 1), spec(1, P), spec(1, P), spec(1, P), spec(1, P),
                  spec(C, P), spec(C, P), spec(P, SSM_TC), spec(P, SSM_TC)],
        out_specs=[spec(SSM_TC, SSM_TC), spec(SSM_TC, P), spec(SSM_TC, P), spec(P, SSM_TC), spec(P, SSM_TC),
                   spec(1, P), spec(1, P)],
        out_shape=[
            jax.ShapeDtypeStruct((2, G, SSM_TC, SSM_TC), bf16),
            jax.ShapeDtypeStruct((2, G, SSM_TC, P), bf16),
            jax.ShapeDtypeStruct((2, G, SSM_TC, P), bf16),
            jax.ShapeDtypeStruct((2, G, P, SSM_TC), bf16),
            jax.ShapeDtypeStruct((2, G, P, SSM_TC), bf16),
            jax.ShapeDtypeStruct((2, G, 1, P), f32),
            jax.ShapeDtypeStruct((2, G, 1, P), f32),
        ],
        name="ssm_prep",
    )(col(a_re), col(a_im), rowv(a_re), rowv(a_im), rowv(lam_re), rowv(lam_im),
      bt(b_re), bt(b_im), ct(c_re), ct(c_im))


LANES = 128
GPT = LANES // SSM_CH


def _shift_chunks(a, k, chunk_id, nchunk, down):
    m = a.shape[0]
    if down:
        return jnp.where(chunk_id >= k, pltpu.roll(a, k, 0), 0.0)
    return jnp.where(chunk_id < nchunk - k, pltpu.roll(a, m - k, 0), 0.0)


def _ssm_kernel(*refs, batch, nchunk, has_h0):
    if has_h0:
        (x_ref, km_ref, bpr_ref, bpi_ref, cpr_ref, cpi_ref, ltr_ref, lti_ref, d_ref, h0_ref, y_ref,
         abuf, ybuf) = refs
    else:
        (x_ref, km_ref, bpr_ref, bpi_ref, cpr_ref, cpi_ref, ltr_ref, lti_ref, d_ref, y_ref, fin_ref,
         abuf, ybuf, sbuf) = refs
    gg = pl.program_id(1)
    m = batch * nchunk
    slot = jnp.right_shift(lax.broadcasted_iota(jnp.int32, (1, LANES), 1), int(math.log2(SSM_CH)))

    @pl.when(gg == 0)
    def _():
        for s in range(SSM_T):
            a_s = x_ref[pl.ds(s, m, stride=SSM_T), :]
            ss = s % GPT
            abuf[s] = pltpu.roll(a_s, ss * SSM_CH, 1) if ss else a_s

    tiles = []
    for hh in range(SSM_T // GPT):
        acc = jnp.zeros((m, LANES), f32)
        for ss in range(GPT):
            acc = jnp.where(slot == ((gg + ss) & (GPT - 1)), abuf[hh * GPT + ss], acc)
        tiles.append(pltpu.roll(acc, (LANES - gg * SSM_CH) & (LANES - 1), 1))
    xb = jnp.concatenate(tiles, axis=1).astype(bf16)

    row = lax.broadcasted_iota(jnp.int32, (m, SSM_P), 0)
    chunk_id = row & (nchunk - 1)
    y = None
    for d in range(2):
        down = d == 0
        yk = _dot(xb, km_ref[d])
        y = yk if y is None else y + yk
        e_re = _dot(xb, bpr_ref[d])
        e_im = _dot(xb, bpi_ref[d])
        a_re, a_im = ltr_ref[d], lti_ref[d]
        entry = chunk_id == (0 if down else nchunk - 1)
        if has_h0:
            h_re_t = jnp.zeros((m, SSM_P), f32)
            h_im_t = jnp.zeros((m, SSM_P), f32)
            for b in range(batch):
                mine = jnp.right_shift(row, int(math.log2(nchunk))) == b
                h_re_t = jnp.where(mine, h0_ref[d, 0, b:b + 1, :], h_re_t)
                h_im_t = jnp.where(mine, h0_ref[d, 1, b:b + 1, :], h_im_t)
            e_re = e_re + jnp.where(entry, a_re * h_re_t - a_im * h_im_t, 0.0)
            e_im = e_im + jnp.where(entry, a_re * h_im_t + a_im * h_re_t, 0.0)
        s_re, s_im = e_re, e_im
        step = 1
        while step < nchunk:
            p_re = _shift_chunks(s_re, step, chunk_id, nchunk, down)
            p_im = _shift_chunks(s_im, step, chunk_id, nchunk, down)
            s_re, s_im = (s_re + a_re * p_re - a_im * p_im, s_im + a_re * p_im + a_im * p_re)
            a_re, a_im = a_re * a_re - a_im * a_im, 2.0 * a_re * a_im
            step *= 2
        hs_re = _shift_chunks(s_re, 1, chunk_id, nchunk, down)
        hs_im = _shift_chunks(s_im, 1, chunk_id, nchunk, down)
        if has_h0:
            hs_re = jnp.where(entry, h_re_t, hs_re)
            hs_im = jnp.where(entry, h_im_t, hs_im)
        else:
            sbuf[0] = s_re
            sbuf[1] = s_im
            last = nchunk - 1 if down else 0
            fin_ref[d, 0] = sbuf[0, pl.ds(last, batch, stride=nchunk), :]
            fin_ref[d, 1] = sbuf[1, pl.ds(last, batch, stride=nchunk), :]
        y = y + _dot(hs_re.astype(bf16), cpr_ref[d]) + _dot(hs_im.astype(bf16), cpi_ref[d])

    up = gg * SSM_CH
    ybuf[gg] = jnp.concatenate([pltpu.roll(y[:, :LANES], up, 1), pltpu.roll(y[:, LANES:], up, 1)], axis=1)

    @pl.when(gg == GPT - 1)
    def _():
        for hh in range(SSM_T // GPT):
            for tt in range(GPT):
                t = hh * GPT + tt
                acc = jnp.zeros((m, LANES), f32)
                for g2 in range(GPT):
                    acc = jnp.where(slot == ((g2 + tt) & (GPT - 1)), ybuf[g2, :, hh * LANES:(hh + 1) * LANES], acc)
                d_t = pltpu.roll(d_ref[...], tt * SSM_CH, 1) if tt else d_ref[...]
                tot = acc + d_t * abuf[t]
                y_ref[pl.ds(t, m, stride=SSM_T), :] = pltpu.roll(tot, LANES - tt * SSM_CH, 1) if tt else tot


def _ssm_call(x2d, mats, d_row, h0, batch, nchunk):
    G, P = SSM_GROUPS, SSM_P
    m = batch * nchunk
    rows = m * SSM_T
    km, bpr, bpi, cpr, cpi, ltr, lti = mats
    spec = lambda r, c_: pl.BlockSpec((2, None, r, c_), lambda q, gg: (0, q * GPT + gg, 0, 0))
    tile = pl.BlockSpec((rows, LANES), lambda q, gg: (0, q))
    in_specs = [
        tile,
        spec(SSM_TC, SSM_TC), spec(SSM_TC, P), spec(SSM_TC, P), spec(P, SSM_TC), spec(P, SSM_TC),
        spec(1, P), spec(1, P),
        pl.BlockSpec((1, LANES), lambda q, gg: (0, q)),
    ]
    args = [x2d, km, bpr, bpi, cpr, cpi, ltr, lti, d_row]
    out_specs = [tile]
    out_shape = [jax.ShapeDtypeStruct((rows, G * SSM_CH), f32)]
    scratch = [pltpu.VMEM((SSM_T, m, LANES), f32), pltpu.VMEM((GPT, m, SSM_TC), f32)]
    st_spec = pl.BlockSpec((None, 2, 2, batch, P), lambda q, gg: (q * GPT + gg, 0, 0, 0, 0))
    if h0 is not None:
        in_specs.append(st_spec)
        args.append(h0)
    else:
        out_specs.append(st_spec)
        out_shape.append(jax.ShapeDtypeStruct((G, 2, 2, batch, P), f32))
        scratch.append(pltpu.VMEM((2, m, P), f32))
    return pl.pallas_call(
        functools.partial(_ssm_kernel, batch=batch, nchunk=nchunk, has_h0=h0 is not None),
        grid=(G // GPT, GPT),
        in_specs=in_specs,
        out_specs=out_specs,
        out_shape=out_shape,
        scratch_shapes=scratch,
        compiler_params=pltpu.CompilerParams(dimension_semantics=("arbitrary", "arbitrary")),
        name="ssm_latent" if h0 is not None else "ssm_context",
    )(*args)


def _attention(q, keys, sink_ref, masks):
    rows = q.shape[0]
    lane = lax.broadcasted_iota(jnp.int32, (1, KV_W), 1)
    half = [lane < HEAD_DIM, lane >= HEAD_DIM]
    kb = []
    for k, v in keys:
        k_sw = pltpu.roll(k, HEAD_DIM, 1)
        v_sw = pltpu.roll(v, HEAD_DIM, 1)
        kb.append(((k.astype(bf16), k_sw.astype(bf16)), (v, v_sw)))
    outs = []
    for pair in range(N_HEADS // 2):
        q_pair = q[:, pair * KV_W:(pair + 1) * KV_W]
        o_pair = None
        for pp in range(2):
            hd = 2 * pair + pp
            kh = hd // Q_PER_KV
            src = 0 if kh == pp else 1
            qh = jnp.where(half[pp], q_pair, 0.0).astype(bf16)
            sink = sink_ref[hd]
            scores = []
            m = jnp.zeros((rows, 1), f32) + sink
            for (kk, _), mask in zip(kb, masks):
                s = lax.dot_general(qh, kk[src], (((1,), (1,)), ((), ())), preferred_element_type=f32)
                if mask is not None:
                    s = s + mask
                scores.append(s)
                m = jnp.maximum(m, jnp.max(s, axis=-1, keepdims=True))
            den = jnp.exp(sink - m)
            acc = None
            for s, (_, vv) in zip(scores, kb):
                e = jnp.exp(s - m)
                den = den + jnp.sum(e, axis=-1, keepdims=True)
                o = _dot(e.astype(bf16), jnp.where(half[pp], vv[src], 0.0).astype(bf16))
                acc = o if acc is None else acc + o
            acc = acc / den
            o_pair = acc if o_pair is None else o_pair + acc
        outs.append(o_pair)
    return jnp.concatenate(outs, axis=-1)


def _p3_kernel(*refs, latent, final, seq_len):
    it = iter(refs)
    x_ref, g_ref, mod_ref, fg_ref, wq_ref, wg_ref, wm_ref = (next(it) for _ in range(7))
    if latent:
        kp_ref, kc_ref, kn_ref, vp_ref, vc_ref, vn_ref, ck_ref, cv_ref, cos_ref, sin_ref = (next(it) for _ in range(10))
    else:
        kc_ref, vc_ref = next(it), next(it)
    sink_ref = next(it)
    oc_ref, op_ref, ys_ref, glu_ref, wbr_ref, wo_ref = (next(it) for _ in range(6))
    o_ref = next(it)

    t0 = pl.program_id(1) * TM

    x = x_ref[...]
    mod = mod_ref[...]
    hb = _modnorm(x, g_ref[...], mod).astype(bf16)

    q = _dot(hb, wq_ref[...]) * HEAD_DIM ** -0.5
    if latent:
        q = _rope(q, cos_ref[...], sin_ref[...])
        k_loc = jnp.concatenate([kp_ref[...], kc_ref[...], kn_ref[...]], axis=0)
        v_loc = jnp.concatenate([vp_ref[...], vc_ref[...], vn_ref[...]], axis=0)
        span = TM + 2 * WINDOW
        qpos = t0 + lax.broadcasted_iota(jnp.int32, (TM, span), 0)
        kpos = t0 - WINDOW + lax.broadcasted_iota(jnp.int32, (TM, span), 1)
        band = (jnp.abs(qpos - kpos) <= WINDOW) & (kpos >= 0) & (kpos < seq_len)
        bias = jnp.where(band, 0.0, NEG_INF)
        o_att = _attention(q, [(ck_ref[...], cv_ref[...]), (k_loc, v_loc)], sink_ref, [None, bias])
    else:
        o_att = _attention(q, [(kc_ref[...], vc_ref[...])], sink_ref, [None])

    gates = _dot(hb, wg_ref[...])

    z = _dot(jax.nn.gelu(ys_ref[...]).astype(bf16), glu_ref[...])
    o_ssm = z[:, :BR_W] * jax.nn.sigmoid(z[:, BR_W:])

    merged = None
    for n, o_n in enumerate((o_att, oc_ref[...], op_ref[...], o_ssm)):
        br = (o_n * _silu(gates[:, n * BR_W:(n + 1) * BR_W])).astype(bf16)
        proj = _dot(br, wbr_ref[n])
        mg = jax.nn.sigmoid(_dot(hb, wm_ref[:, n * D_MODEL:(n + 1) * D_MODEL]))
        merged = mg * proj if merged is None else merged + mg * proj
    out = x + mod[:, 2 * D_MODEL:] * _dot(merged.astype(bf16), wo_ref[...])
    if final:
        out = out * lax.rsqrt(jnp.mean(out * out, axis=-1, keepdims=True) + EPS) * fg_ref[...]
    o_ref[...] = out


def _p3_call(x, norm_g, mod, final_g, wq, wg, wm, k, v, ctx_kv, rope_q, sink, o_conv, o_pool, ys, glu_w,
             w_br, w_out, layer, latent, final):
    B, L, _ = x.shape
    nt = L // TM
    if latent:
        mod_idx = lambda b, j: (layer * MOD_ROWS + 1 + b, 0, 0)
    else:
        mod_idx = lambda b, j: (layer * MOD_ROWS, 0, 0)
    row = lambda w: pl.BlockSpec((None, TM, w), lambda b, j: (b, j, 0))

    def halo(rows, w, side):
        per = TM // rows
        last = L // rows - 1
        if side < 0:
            return pl.BlockSpec((None, rows, w), lambda b, j: (b, jnp.maximum(j * per - 1, 0), 0))
        return pl.BlockSpec((None, rows, w), lambda b, j: (b, jnp.minimum((j + 1) * per, last), 0))

    in_specs = [row(D_MODEL), _const_spec((1, D_MODEL)), pl.BlockSpec((None, 1, 3 * D_MODEL), mod_idx),
                _const_spec((1, D_MODEL)), _const_spec(wq.shape), _const_spec(wg.shape), _const_spec(wm.shape)]
    args = [x, norm_g, mod, final_g, wq, wg, wm]
    if latent:
        ck, cv = ctx_kv
        kv_c = pl.BlockSpec((None, ck.shape[1], KV_W), lambda b, j: (b, 0, 0))
        in_specs += [halo(WINDOW, KV_W, -1), row(KV_W), halo(WINDOW, KV_W, 1),
                     halo(WINDOW, KV_W, -1), row(KV_W), halo(WINDOW, KV_W, 1), kv_c, kv_c,
                     pl.BlockSpec((TM, N_HEADS * HEAD_DIM), lambda b, j: (j, 0)),
                     pl.BlockSpec((TM, N_HEADS * HEAD_DIM), lambda b, j: (j, 0))]
        args += [k, k, k, v, v, v, ck, cv, rope_q[0], rope_q[1]]
    else:
        in_specs += [row(KV_W), row(KV_W)]
        args += [k, v]
    in_specs.append(pl.BlockSpec(memory_space=pltpu.SMEM))
    args.append(sink)
    in_specs += [row(BR_W), row(BR_W), row(BR_W)]
    args += [o_conv, o_pool, ys]
    tail = [glu_w, w_br, w_out]
    in_specs += [_const_spec(a.shape) for a in tail]
    args += tail
    return pl.pallas_call(
        functools.partial(_p3_kernel, latent=latent, final=final, seq_len=L),
        grid=(B, nt),
        in_specs=in_specs,
        out_specs=row(D_MODEL),
        out_shape=jax.ShapeDtypeStruct((B, L, D_MODEL), f32),
        compiler_params=pltpu.CompilerParams(
            dimension_semantics=("arbitrary", "arbitrary"), vmem_limit_bytes=V7X_VMEM_LIMIT),
        name="p3_latent" if latent else "p3_context",
    )(*args)


def _rope_tables(L, heads):
    rows = L // GRID_W
    row = jnp.repeat(jnp.arange(rows, dtype=f32), GRID_W)
    col = jnp.tile(jnp.arange(GRID_W, dtype=f32), rows)
    n_freq = HEAD_DIM // 4
    inv = ROPE_THETA ** (-jnp.arange(n_freq, dtype=f32) / n_freq)
    ang = jnp.concatenate([row[:, None] * inv, col[:, None] * inv], axis=-1)
    cos, sin = jnp.cos(ang), jnp.sin(ang)
    return (jnp.tile(jnp.concatenate([cos, cos], axis=-1), (1, heads)),
            jnp.tile(jnp.concatenate([-sin, sin], axis=-1), (1, heads)))


def _cols(w, *names):
    return jnp.concatenate([w[:, IN_OFF[n][0]:IN_OFF[n][1]] for n in names], axis=1).astype(bf16)


def kernel(x_prompt, x_sample, cache_k, cache_v, state_ssm, c, c_ctx, norm_g, w_ada, b_ada, w_in, attn_sink, conv_dw, conv_db, conv_ln_g, conv_ln_b, pool_w, pool_scale, ssm_lam_re, ssm_lam_im, ssm_log_dt, ssm_b_re, ssm_b_im, ssm_c_re, ssm_c_im, ssm_d, ssm_glu_w, w_br, w_out, final_g):
    BP, LP, _ = x_prompt.shape
    BS, LS, _ = x_sample.shape
    PAST = cache_k.shape[2]
    c_rows = jnp.concatenate([c_ctx[None, :], c, jnp.zeros((MOD_ROWS - 1 - BS, D_MODEL), f32)], axis=0)
    mod = _mod_call(c_rows, w_ada, b_ada).reshape(DEPTH * MOD_ROWS, 1, 3 * D_MODEL)
    rope_k = _rope_tables(LS, N_KV)
    rope_q = _rope_tables(LS, N_HEADS)
    fg = final_g.reshape(1, D_MODEL)

    xp, xs = x_prompt, x_sample
    new_k, new_v, new_st = [], [], []
    for l in range(DEPTH):
        ng = norm_g[l].reshape(1, D_MODEL)
        w1 = _cols(w_in[l], "k", "v", "a_conv", "x_pool", "x_ssm")
        wq = _cols(w_in[l], "q")
        wg = _cols(w_in[l], "g_att", "g_conv", "g_pool", "g_ssm")
        wm = _cols(w_in[l], "g_mrg")
        mats = _ssm_prep_call(ssm_lam_re[l], ssm_lam_im[l], ssm_log_dt[l], ssm_b_re[l], ssm_b_im[l],
                              ssm_c_re[l], ssm_c_im[l])
        d_row = ssm_d[l].reshape(1, BR_W)
        dw = jnp.concatenate([conv_dw[l], jnp.zeros((1, BR_W), f32)], axis=0)
        conv = (dw, conv_db[l].reshape(1, BR_W), conv_ln_g[l].reshape(1, BR_W), conv_ln_b[l].reshape(1, BR_W))
        pool = (pool_w[l].astype(bf16), pool_scale[l].reshape(1, BR_W))
        glu = ssm_glu_w[l].astype(bf16)
        wbr = w_br[l].astype(bf16)
        wo = w_out[l].astype(bf16)
        final = l == DEPTH - 1

        k, v, oc, opl, xss = _p1_call(xp, ng, mod, w1, conv, pool, l, False, None)
        yg, fin = _ssm_call(xss.reshape(BP * LP, BR_W), mats, d_row, None, BP, LP // SSM_T)
        ys = yg.reshape(BP, LP, BR_W)
        xp = _p3_call(xp, ng, mod, fg, wq, wg, wm, k, v, None, None, attn_sink[l], oc, opl, ys,
                      glu, wbr, wo, l, False, final)
        new_k.append(k.reshape(BP, LP, N_KV, HEAD_DIM))
        new_v.append(v.reshape(BP, LP, N_KV, HEAD_DIM))
        new_st.append(jnp.transpose(fin, (3, 1, 2, 0, 4)))

        k, v, oc, opl, xss = _p1_call(xs, ng, mod, w1, conv, pool, l, True, rope_k)
        h0 = jnp.transpose(state_ssm[:, l], (3, 1, 2, 0, 4))
        (yg,) = _ssm_call(xss.reshape(BS * LS, BR_W), mats, d_row, h0, BS, LS // SSM_T)
        ys = yg.reshape(BS, LS, BR_W)
        ctx = (cache_k[:, l].reshape(BS, PAST, KV_W), cache_v[:, l].reshape(BS, PAST, KV_W))
        xs = _p3_call(xs, ng, mod, fg, wq, wg, wm, k, v, ctx, rope_q, attn_sink[l], oc, opl, ys,
                      glu, wbr, wo, l, True, final)

    return (xp, xs, jnp.stack(new_k, axis=1), jnp.stack(new_v, axis=1), jnp.stack(new_st, axis=1))
```
